```python
import math
import jax, jax.numpy as jnp
from jax import lax
import numpy as np

D_MODEL = 1024
BATCH = 8
SEQ = 4096
DEPTH = 1
DEC_BATCH = 128
DEC_SEQ = 8
PAST_LEN = 8192
PAGE_SIZE = 128

HEAD_DIM = 64
H_A = 8
H_B = 8
W_A = H_A * HEAD_DIM
W_B = H_B * HEAD_DIM
D_FF = 4 * D_MODEL
MOBA_BLOCK = 256
MOBA_TOPK = 3
MOBA_Q_BLOCK = 64
SB_Q_BLOCK = 128
N_BUCKETS = 32
MAX_DISTANCE = 128
EPS = 1e-6
SCALE = HEAD_DIM ** -0.5
IN_COLS = 3 * W_A + 3 * W_B + 2 * D_MODEL

kernel_name = "moba_stickbreaking_gated_hybrid_step"


def rms_norm(x, g):
    xf = x.astype(jnp.float32)
    y = xf * lax.rsqrt(jnp.mean(xf * xf, axis=-1, keepdims=True) + EPS)
    return (y * g.astype(jnp.float32)).astype(x.dtype)


def t5_bucket(dist):
    max_exact = N_BUCKETS // 2
    d = jnp.maximum(dist, 0)
    df = jnp.maximum(d, 1).astype(jnp.float32)
    large = max_exact + (jnp.log(df / max_exact) / math.log(MAX_DISTANCE / max_exact)
                         * (N_BUCKETS - max_exact)).astype(jnp.int32)
    large = jnp.minimum(large, N_BUCKETS - 1)
    return jnp.where(d < max_exact, d, large)


def mixer_proj(x, norm_g, w_in, qn, kn):
    h = rms_norm(x, norm_g)
    p = h @ w_in
    cuts = [W_A, 2 * W_A, 3 * W_A, 3 * W_A + W_B, 3 * W_A + 2 * W_B, 3 * W_A + 3 * W_B,
            3 * W_A + 3 * W_B + D_MODEL]
    q_a, k_a, v_a, q_b, k_b, v_b, g_a, g_b = jnp.split(p, cuts, axis=-1)
    heads = lambda t, n: t.reshape(t.shape[:-1] + (n, HEAD_DIM))
    q_a = rms_norm(heads(q_a, H_A), qn)
    k_a = rms_norm(heads(k_a, H_A), kn)
    return (q_a, k_a, heads(v_a, H_A), heads(q_b, H_B), heads(k_b, H_B), heads(v_b, H_B), g_a, g_b)


def moba_blocks(k, v):
    H, L, DH = k.shape
    nb = max(-(-L // MOBA_BLOCK), MOBA_TOPK)
    pad = nb * MOBA_BLOCK - L
    kb = jnp.pad(k, ((0, 0), (0, pad), (0, 0))).reshape(H, nb, MOBA_BLOCK, DH)
    vb = jnp.pad(v, ((0, 0), (0, pad), (0, 0))).reshape(H, nb, MOBA_BLOCK, DH)
    km = jnp.mean(kb.astype(jnp.float32), axis=2).astype(k.dtype)
    return kb, vb, km


def moba_attend(q, q_pos, k_blk, v_blk, k_mean, rel_bias):
    H, Q, _ = q.shape
    nb = k_blk.shape[1]
    own = q_pos // MOBA_BLOCK
    gate = jnp.einsum('hqd,hnd->hqn', q, k_mean).astype(jnp.float32)
    past_ok = jnp.arange(nb)[None, :] < own[:, None]
    gate = jnp.where(past_ok[None], gate, -jnp.inf)
    _, top = lax.top_k(gate, MOBA_TOPK)
    top_ok = jnp.arange(MOBA_TOPK)[None, :] < own[:, None]
    sel = jnp.concatenate([top, jnp.broadcast_to(own[None, :, None], (H, Q, 1))], axis=-1)
    h_idx = jnp.arange(H)[:, None, None]
    k_sel = k_blk[h_idx, sel]
    v_sel = v_blk[h_idx, sel]
    k_pos = sel[..., None] * MOBA_BLOCK + jnp.arange(MOBA_BLOCK)
    dist = q_pos[None, :, None, None] - k_pos
    slot_ok = jnp.concatenate([top_ok, jnp.ones((Q, 1), bool)], axis=-1)
    valid = slot_ok[None, :, :, None] & (dist >= 0)
    bias = rel_bias.T[jnp.arange(H)[:, None, None, None], t5_bucket(dist)].astype(jnp.float32)
    s = jnp.einsum('hqd,hqnkd->hqnk', q, k_sel).astype(jnp.float32) * SCALE + bias
    s = jnp.where(valid, s, -jnp.inf)
    p = jax.nn.softmax(s.reshape(H, Q, -1), axis=-1).reshape(s.shape)
    return jnp.einsum('hqnk,hqnkd->hqd', p.astype(v_sel.dtype), v_sel)


def stick_breaking_attend(q, q_pos, k, v, k_pos):
    z = jnp.einsum('hqd,hkd->hqk', q, k).astype(jnp.float32) * SCALE
    past = k_pos[None, :] < q_pos[:, None]
    log_beta = jax.nn.log_sigmoid(z)
    log_keep = jnp.where(past, log_beta - z, 0.0)
    after = lax.cumsum(log_keep, axis=2, reverse=True) - log_keep
    a = jnp.where(past, jnp.exp(log_beta + after), 0.0)
    return jnp.einsum('hqk,hkd->hqd', a.astype(v.dtype), v)


def prompt_mixers(q_a, k_a, v_a, q_b, k_b, v_b, rel_bias):
    B, S = q_a.shape[:2]
    pos = jnp.arange(S, dtype=jnp.int32)
    th = lambda t: jnp.swapaxes(t, 1, 2)
    nq_a = S // MOBA_Q_BLOCK

    def moba_seq(args):
        qs, ks, vs = args
        kb, vb, km = moba_blocks(ks, vs)
        qc = qs.reshape(H_A, nq_a, MOBA_Q_BLOCK, HEAD_DIM).transpose(1, 0, 2, 3)
        pc = pos.reshape(nq_a, MOBA_Q_BLOCK)
        oc = lax.map(lambda a: moba_attend(a[0], a[1], kb, vb, km, rel_bias), (qc, pc))
        return oc.transpose(1, 0, 2, 3).reshape(H_A, S, HEAD_DIM)

    o_a = lax.map(moba_seq, (th(q_a), th(k_a), th(v_a)))
    o_a = jnp.swapaxes(o_a, 1, 2).reshape(B, S, W_A)

    nq_b = S // SB_Q_BLOCK
    qb = th(q_b).reshape(B, H_B, nq_b, SB_Q_BLOCK, HEAD_DIM).transpose(2, 0, 1, 3, 4)
    pb = pos.reshape(nq_b, SB_Q_BLOCK)
    kt, vt = th(k_b), th(v_b)
    sb = jax.vmap(stick_breaking_attend, in_axes=(0, None, 0, 0, None))
    o_b = lax.map(lambda a: sb(a[0], a[1], kt, vt, pos), (qb, pb))
    o_b = o_b.transpose(1, 0, 3, 2, 4).reshape(B, S, W_B)
    return o_a, o_b


def sample_mixers(q_a, k_a, v_a, q_b, k_b, v_b, ck_a, cv_a, ck_b, cv_b, page_table, rel_bias):
    Bd, T = q_a.shape[:2]
    n_pages = page_table.shape[1]
    past_len = n_pages * PAGE_SIZE
    q_pos = past_len + jnp.arange(T, dtype=jnp.int32)
    k_pos = jnp.arange(past_len + T, dtype=jnp.int32)

    def per_seq(args):
        pt, qa, ka, va, qb, kb, vb = args

        def full(pool, new):
            past = pool[pt].reshape(past_len, new.shape[1], HEAD_DIM)
            return jnp.swapaxes(jnp.concatenate([past, new.astype(past.dtype)], axis=0), 0, 1)

        kab, vab, kam = moba_blocks(full(ck_a, ka), full(cv_a, va))
        oa = moba_attend(jnp.swapaxes(qa, 0, 1), q_pos, kab, vab, kam, rel_bias)
        ob = stick_breaking_attend(jnp.swapaxes(qb, 0, 1), q_pos, full(ck_b, kb), full(cv_b, vb), k_pos)
        return jnp.swapaxes(oa, 0, 1), jnp.swapaxes(ob, 0, 1)

    o_a, o_b = lax.map(per_seq, (page_table, q_a, k_a, v_a, q_b, k_b, v_b))
    return o_a.reshape(Bd, T, W_A), o_b.reshape(Bd, T, W_B)


def merge_and_mlp(x, o_a, o_b, g_a, g_b, w_br_a, w_br_b, w_o, norm_mlp, w_up, w_down):
    mixed = jax.nn.sigmoid(g_a) * (o_a @ w_br_a) + jax.nn.sigmoid(g_b) * (o_b @ w_br_b)
    x = x + mixed @ w_o
    h = rms_norm(x, norm_mlp)
    return x + jnp.square(jax.nn.relu(h @ w_up)) @ w_down


def setup_inputs(seed: int = 0) -> dict:
    key = jax.random.key(seed)
    ks = jax.random.split(key, 20)
    f32 = jnp.float32
    n_pages = PAST_LEN // PAGE_SIZE
    used = DEC_BATCH * n_pages
    n_phys = used + max(used // 4, 1)
    nrm = lambda k, shape, s: jax.random.normal(k, shape, f32) * s
    page_table = jax.random.permutation(ks[6], n_phys)[:used].reshape(DEC_BATCH, n_pages).astype(jnp.int32)
    return {
        "x_prompt": nrm(ks[0], (BATCH, SEQ, D_MODEL), 1.0),
        "x_sample": nrm(ks[1], (DEC_BATCH, DEC_SEQ, D_MODEL), 1.0),
        "cache_k_a": nrm(ks[2], (DEPTH, n_phys, PAGE_SIZE, H_A, HEAD_DIM), 1.0),
        "cache_v_a": nrm(ks[3], (DEPTH, n_phys, PAGE_SIZE, H_A, HEAD_DIM), 1.0),
        "cache_k_b": nrm(ks[4], (DEPTH, n_phys, PAGE_SIZE, H_B, HEAD_DIM), 1.0),
        "cache_v_b": nrm(ks[5], (DEPTH, n_phys, PAGE_SIZE, H_B, HEAD_DIM), 1.0),
        "page_table": page_table,
        "rel_bias": nrm(ks[7], (N_BUCKETS, H_A), 0.5),
        "norm_attn": 1.0 + nrm(ks[8], (DEPTH, D_MODEL), 0.05),
        "w_in": nrm(ks[9], (DEPTH, D_MODEL, IN_COLS), D_MODEL ** -0.5),
        "q_norm": 1.0 + nrm(ks[10], (DEPTH, HEAD_DIM), 0.05),
        "k_norm": 1.0 + nrm(ks[11], (DEPTH, HEAD_DIM), 0.05),
        "w_br_a": nrm(ks[12], (DEPTH, W_A, D_MODEL), W_A ** -0.5),
        "w_br_b": nrm(ks[13], (DEPTH, W_B, D_MODEL), W_B ** -0.5),
        "w_o": nrm(ks[14], (DEPTH, D_MODEL, D_MODEL), D_MODEL ** -0.5),
        "norm_mlp": 1.0 + nrm(ks[15], (DEPTH, D_MODEL), 0.05),
        "w_up": nrm(ks[16], (DEPTH, D_MODEL, D_FF), D_MODEL ** -0.5),
        "w_down": nrm(ks[17], (DEPTH, D_FF, D_MODEL), D_FF ** -0.5),
    }


def reference(x_prompt, x_sample, cache_k_a, cache_v_a, cache_k_b, cache_v_b, page_table,
              rel_bias, norm_attn, w_in, q_norm, k_norm, w_br_a, w_br_b, w_o, norm_mlp, w_up, w_down):
    xp, xs = x_prompt, x_sample
    pka, pva, pkb, pvb = [], [], [], []
    ska, sva, skb, svb = [], [], [], []
    for layer in range(DEPTH):
        qa, ka, va, qb, kb, vb, ga, gb = mixer_proj(xp, norm_attn[layer], w_in[layer], q_norm[layer], k_norm[layer])
        oa, ob = prompt_mixers(qa, ka, va, qb, kb, vb, rel_bias)
        xp = merge_and_mlp(xp, oa, ob, ga, gb, w_br_a[layer], w_br_b[layer], w_o[layer],
                           norm_mlp[layer], w_up[layer], w_down[layer])
        pka.append(ka); pva.append(va); pkb.append(kb); pvb.append(vb)
        qa, ka, va, qb, kb, vb, ga, gb = mixer_proj(xs, norm_attn[layer], w_in[layer], q_norm[layer], k_norm[layer])
        oa, ob = sample_mixers(qa, ka, va, qb, kb, vb, cache_k_a[layer], cache_v_a[layer],
                               cache_k_b[layer], cache_v_b[layer], page_table, rel_bias)
        xs = merge_and_mlp(xs, oa, ob, ga, gb, w_br_a[layer], w_br_b[layer], w_o[layer],
                           norm_mlp[layer], w_up[layer], w_down[layer])
        ska.append(ka); sva.append(va); skb.append(kb); svb.append(vb)
    return (xp, xs, jnp.stack(pka), jnp.stack(pva), jnp.stack(pkb), jnp.stack(pvb),
            jnp.stack(ska), jnp.stack(sva), jnp.stack(skb), jnp.stack(svb))
```

```python
import functools
import math

import jax
import jax.numpy as jnp
import numpy as np
from jax import lax
from jax.experimental import pallas as pl
from jax.experimental.pallas import tpu as pltpu

F32 = jnp.float32
BF16 = jnp.bfloat16

HEAD_DIM = 64
MOBA_BLOCK = 256
MOBA_TOPK = 3
N_BUCKETS = 32
MAX_DISTANCE = 128
EPS = 1e-6
SCALE = HEAD_DIM ** -0.5
PAGE_SIZE = 128

LANES = 128
HEADS_PER_TILE = LANES // HEAD_DIM
TOKEN_TILE = 256
PAGES_PER_STEP = 8
PAGES_PER_BLOCK = MOBA_BLOCK // PAGE_SIZE
VMEM_LIMIT = 56 * 1024 * 1024

NT_DIMS = (((1,), (1,)), ((), ()))


def _dot(a, b):
    return jnp.dot(a, b, preferred_element_type=F32)


def _dot_nt(a, b):
    return lax.dot_general(a, b, NT_DIMS, preferred_element_type=F32)


def _split_bf16(x):
    hi = x.astype(BF16)
    lo = (x - hi.astype(F32)).astype(BF16)
    return hi, lo


def _t5_bucket_np(dist):
    max_exact = N_BUCKETS // 2
    d = np.maximum(dist, 0)
    df = np.maximum(d, 1).astype(np.float32)
    large = max_exact + (np.log(df / max_exact) / math.log(MAX_DISTANCE / max_exact)
                         * (N_BUCKETS - max_exact)).astype(np.int32)
    large = np.minimum(large, N_BUCKETS - 1)
    return np.where(d < max_exact, d, large).astype(np.int32)


def _compiler_params(semantics):
    return pltpu.CompilerParams(dimension_semantics=semantics, vmem_limit_bytes=VMEM_LIMIT)


def _const_spec(shape):
    zeros = (0,) * len(shape)
    return pl.BlockSpec(shape, lambda *_: zeros, pipeline_mode=pl.Buffered(1))


def _proj_kernel(x_ref, ng_ref, w_ref, gmat_ref, qn_ref, kn_ref,
                 ka_ref, va_ref, kb_ref, vb_ref, ga_ref, gb_ref,
                 qah_ref, qal_ref, kab_ref, vab_ref, qbb_ref, kbb_ref, vbb_ref, km_ref,
                 *, w_a, w_b, d_model):
    x = x_ref[...]
    ms = jnp.mean(x * x, axis=-1, keepdims=True)
    hb = (x * lax.rsqrt(ms + EPS) * ng_ref[...]).astype(BF16)

    def seg(c0, n):
        return _dot(hb, w_ref[:, c0:c0 + n])

    def head_norm(t, g_ref):
        hi, lo = _split_bf16(t * t)
        gm = gmat_ref[...]
        msq = _dot(hi, gm) + _dot(lo, gm)
        return t * lax.rsqrt(msq + EPS) * g_ref[...]

    c = 0
    qa = head_norm(seg(c, w_a), qn_ref); c += w_a
    qa = qa * SCALE
    qah, qal = _split_bf16(qa)
    qah_ref[...] = qah
    qal_ref[...] = qal
    ka = head_norm(seg(c, w_a), kn_ref); c += w_a
    ka_ref[...] = ka
    kab_ref[...] = ka.astype(BF16)
    km_ref[0] = jnp.mean(ka, axis=0, keepdims=True)
    va = seg(c, w_a); c += w_a
    va_ref[...] = va
    vab_ref[...] = va.astype(BF16)
    qb = seg(c, w_b); c += w_b
    qbb_ref[...] = (qb * SCALE).astype(BF16)
    kb = seg(c, w_b); c += w_b
    kb_ref[...] = kb
    kbb_ref[...] = kb.astype(BF16)
    vb = seg(c, w_b); c += w_b
    vb_ref[...] = vb
    vbb_ref[...] = vb.astype(BF16)
    ga_ref[...] = seg(c, d_model); c += d_model
    gb_ref[...] = seg(c, d_model)


def _proj(x, norm_g, w_in_bf, gmat, qn_t, kn_t, w_a, w_b):
    n, d_model = x.shape
    ts = TOKEN_TILE
    steps = n // ts
    row = lambda w: pl.BlockSpec((ts, w), lambda i: (i, 0))
    f32 = lambda w: jax.ShapeDtypeStruct((n, w), F32)
    bf = lambda w: jax.ShapeDtypeStruct((n, w), BF16)
    out_shape = ([f32(w_a), f32(w_a), f32(w_b), f32(w_b), f32(d_model), f32(d_model)]
                 + [bf(w_a)] * 4 + [bf(w_b)] * 3
                 + [jax.ShapeDtypeStruct((steps, 1, w_a), F32)])
    out_specs = ([row(w_a), row(w_a), row(w_b), row(w_b), row(d_model), row(d_model)]
                 + [row(w_a)] * 4 + [row(w_b)] * 3
                 + [pl.BlockSpec((1, 1, w_a), lambda i: (i, 0, 0))])
    return pl.pallas_call(
        functools.partial(_proj_kernel, w_a=w_a, w_b=w_b, d_model=d_model),
        grid=(steps,),
        in_specs=[row(d_model), _const_spec((1, d_model)), _const_spec(w_in_bf.shape),
                  _const_spec(gmat.shape), _const_spec((1, w_a)), _const_spec((1, w_a))],
        out_specs=out_specs,
        out_shape=out_shape,
        compiler_params=_compiler_params(("parallel",)),
        name="proj",
    )(x, norm_g, w_in_bf, gmat, qn_t, kn_t)


def _prompt_bias_kernel(relb_ref, bkt_ref, out_ref):
    h = pl.program_id(0)
    b = bkt_ref[...]
    acc = jnp.zeros(b.shape, F32)
    for n in range(N_BUCKETS):
        acc = jnp.where(b == n, relb_ref[h, n], acc)
    out_ref[0] = acc


def _prompt_bias(relb_t):
    n_heads = relb_t.shape[0]
    r = np.arange(MOBA_BLOCK)[:, None]
    c = np.arange(MOBA_BLOCK)[None, :]
    bkt = np.stack([_t5_bucket_np(r - c), _t5_bucket_np(MOBA_BLOCK + r - c)])
    return pl.pallas_call(
        _prompt_bias_kernel,
        grid=(n_heads,),
        in_specs=[pl.BlockSpec(memory_space=pltpu.SMEM), _const_spec(bkt.shape)],
        out_specs=pl.BlockSpec((1,) + bkt.shape, lambda h: (h, 0, 0, 0)),
        out_shape=jax.ShapeDtypeStruct((n_heads,) + bkt.shape, F32),
        compiler_params=_compiler_params(("parallel",)),
        name="prompt_bias",
    )(relb_t, jnp.asarray(bkt))


def _sample_bias_kernel(relb_ref, idx_ref, out_ref, *, n_entries):
    idx = idx_ref[...]
    acc = jnp.zeros(idx.shape, F32)
    for n in range(n_entries):
        acc = jnp.where(idx == n, relb_ref[n], acc)
    out_ref[...] = acc


def _sample_bias(relb_flat, n_heads, t_new, past_len):
    rows = n_heads * t_new
    head = (np.arange(rows) // t_new)[:, None]
    t = (np.arange(rows) % t_new)[:, None]
    key = np.arange(MOBA_BLOCK)[None, :]
    far = np.full((rows, MOBA_BLOCK), N_BUCKETS - 1, np.int32)
    last = _t5_bucket_np(past_len + t - (past_len - MOBA_BLOCK + key))
    new = _t5_bucket_np(t - key)
    idx = (np.stack([far, last, new]) + head[None] * N_BUCKETS).astype(np.int32)
    return pl.pallas_call(
        functools.partial(_sample_bias_kernel, n_entries=n_heads * N_BUCKETS),
        in_specs=[pl.BlockSpec(memory_space=pltpu.SMEM), pl.BlockSpec(memory_space=pltpu.VMEM)],
        out_specs=pl.BlockSpec(memory_space=pltpu.VMEM),
        out_shape=jax.ShapeDtypeStruct(idx.shape, F32),
        name="sample_bias",
    )(relb_flat, jnp.asarray(idx))


def _top_blocks(gate, valid, lane, n_cols):
    gate = jnp.where(valid, gate, -jnp.inf)
    rank = jnp.zeros(gate.shape, jnp.int32)
    for j in range(n_cols):
        cj = gate[:, j:j + 1]
        beats = (cj > gate) | ((cj == gate) & (lane > j))
        rank = rank + beats.astype(jnp.int32)
    return valid & (rank < MOBA_TOPK)


def _lane_pick(x, lane, j):
    return jnp.sum(jnp.where(lane == j, x, 0.0), axis=1, keepdims=True)


def _moba_kernel(qh_ref, ql_ref, k_ref, v_ref, km_ref, bias_ref, far_ref, o_ref):
    i = pl.program_id(2)
    tq = qh_ref.shape[1]
    n_blocks = km_ref.shape[1]
    lane = lax.broadcasted_iota(jnp.int32, (1, LANES), 1)
    row = lax.broadcasted_iota(jnp.int32, (tq, MOBA_BLOCK), 0)
    col = lax.broadcasted_iota(jnp.int32, (tq, MOBA_BLOCK), 1)
    causal = col <= row
    blk = lax.broadcasted_iota(jnp.int32, (tq, n_blocks), 1)
    km_hi, km_lo = _split_bf16(km_ref[0])

    def block_kv(j):
        rows = pl.ds(pl.multiple_of(j * MOBA_BLOCK, MOBA_BLOCK), MOBA_BLOCK)
        return k_ref[0, rows, :], v_ref[0, rows, :]

    outs = []
    for hs in range(HEADS_PER_TILE):
        mine = (lane // HEAD_DIM) == hs
        qh = jnp.where(mine, qh_ref[0], 0)
        ql = jnp.where(mine, ql_ref[0], 0)
        gate = _dot_nt(qh, km_hi) + _dot_nt(ql, km_hi) + _dot_nt(qh, km_lo)
        sel = _top_blocks(gate, blk < i, blk, n_blocks).astype(F32)

        def with_sum_lanes(v):
            return jnp.where(mine, v, jnp.ones_like(v))

        kj, vj = block_kv(i)
        s = _dot_nt(qh, kj) + bias_ref[hs, 0]
        s = jnp.where(causal, s, -jnp.inf)
        m0 = jnp.max(s, axis=1, keepdims=True)
        acc0 = _dot(jnp.exp(s - m0).astype(BF16), with_sum_lanes(vj))

        def past_block(j, m, acc, bias):
            kj, vj = block_kv(j)
            s = _dot_nt(qh, kj) + bias
            mb = jnp.max(s, axis=1, keepdims=True)
            chosen = _lane_pick(sel, blk, j) > 0.5
            m_new = jnp.maximum(m, jnp.where(chosen, mb, -jnp.inf))
            m_use = jnp.where(chosen, m_new, jnp.maximum(m_new, mb))
            pv = _dot(jnp.exp(s - m_use).astype(BF16), with_sum_lanes(vj))
            acc = jnp.exp(m - m_new) * acc + jnp.where(chosen, pv, 0.0)
            return m_new, acc

        def far_body(j, carry):
            return past_block(j, carry[0], carry[1], far_ref[hs])

        m1, acc1 = lax.fori_loop(0, jnp.maximum(i - 1, 0), far_body, (m0, acc0))
        m2, acc2 = lax.cond(
            i >= 1,
            lambda m, a: past_block(i - 1, m, a, bias_ref[hs, 1]),
            lambda m, a: (m, a),
            m1, acc1)
        other = (1 - hs) * HEAD_DIM
        outs.append(acc2 / acc2[:, other:other + 1])
    o_ref[0] = jnp.where(lane < HEAD_DIM, outs[0], outs[1]).astype(o_ref.dtype)


def _moba_prompt(qh, ql, kb, vb, km, bias, far):
    b, s, w = qh.shape
    n_tiles = w // LANES
    nq = s // MOBA_BLOCK
    q_spec = pl.BlockSpec((1, MOBA_BLOCK, LANES), lambda bi, hp, i: (bi, i, hp))
    kv_spec = pl.BlockSpec((1, s, LANES), lambda bi, hp, i: (bi, 0, hp))
    return pl.pallas_call(
        _moba_kernel,
        grid=(b, n_tiles, nq),
        in_specs=[q_spec, q_spec, kv_spec, kv_spec,
                  pl.BlockSpec((1, km.shape[1], LANES), lambda bi, hp, i: (bi, 0, hp)),
                  pl.BlockSpec((HEADS_PER_TILE, 2, MOBA_BLOCK, MOBA_BLOCK),
                               lambda bi, hp, i: (hp, 0, 0, 0)),
                  pl.BlockSpec((HEADS_PER_TILE, 1, MOBA_BLOCK), lambda bi, hp, i: (hp, 0, 0))],
        out_specs=q_spec,
        out_shape=jax.ShapeDtypeStruct((b, s, w), BF16),
        compiler_params=_compiler_params(("parallel", "parallel", "arbitrary")),
        name="moba_prompt",
    )(qh, ql, kb, vb, km, bias, far)


def _sb_terms(z):
    sp = jnp.log1p(jnp.exp(-jnp.abs(z)))
    log_beta = jnp.minimum(z, 0.0) - sp
    return log_beta, log_beta - z


def _suffix_sums(log_keep, tri):
    hi, lo = _split_bf16(log_keep)
    return _dot(hi, tri) + _dot(lo, tri)


def _sb_kernel(q_ref, k_ref, v_ref, tri_ref, o_ref):
    i = pl.program_id(2)
    tq = q_ref.shape[1]
    lane = lax.broadcasted_iota(jnp.int32, (1, LANES), 1)
    row = lax.broadcasted_iota(jnp.int32, (tq, tq), 0)
    col = lax.broadcasted_iota(jnp.int32, (tq, tq), 1)
    strict = col < row

    def tile(j, qh, carry, acc, diagonal):
        rows = pl.ds(pl.multiple_of(j * tq, tq), tq)
        kj = k_ref[0, rows, :]
        vj = v_ref[0, rows, :]
        log_beta, log_keep = _sb_terms(_dot_nt(qh, kj))
        if diagonal:
            log_keep = jnp.where(strict, log_keep, 0.0)
        incl = _suffix_sums(log_keep, tri_ref[...])
        a = jnp.exp(log_beta + (incl - log_keep) + carry)
        if diagonal:
            a = jnp.where(strict, a, 0.0)
        return carry + incl[:, 0:1], acc + _dot(a.astype(BF16), vj)

    outs = []
    for hs in range(HEADS_PER_TILE):
        mine = (lane // HEAD_DIM) == hs
        qh = jnp.where(mine, q_ref[0], 0)
        carry, acc = tile(i, qh, jnp.zeros((tq, 1), F32), jnp.zeros((tq, LANES), F32), True)

        def body(t, c, qh=qh):
            return tile(i - 1 - t, qh, c[0], c[1], False)

        carry, acc = lax.fori_loop(0, i, body, (carry, acc))
        outs.append(acc)
    o_ref[0] = jnp.where(lane < HEAD_DIM, outs[0], outs[1]).astype(o_ref.dtype)


def _sb_prompt(q, kb, vb, tri):
    b, s, w = q.shape
    n_tiles = w // LANES
    tq = tri.shape[0]
    q_spec = pl.BlockSpec((1, tq, LANES), lambda bi, hp, i: (bi, i, hp))
    kv_spec = pl.BlockSpec((1, s, LANES), lambda bi, hp, i: (bi, 0, hp))
    return pl.pallas_call(
        _sb_kernel,
        grid=(b, n_tiles, s // tq),
        in_specs=[q_spec, kv_spec, kv_spec, _const_spec(tri.shape)],
        out_specs=q_spec,
        out_shape=jax.ShapeDtypeStruct((b, s, w), BF16),
        compiler_params=_compiler_params(("parallel", "parallel", "arbitrary")),
        name="sb_prompt",
    )(q, kb, vb, tri)


def _sample_kernel(pt_ref, qa_ref, qb_ref, kan_ref, van_ref, kbn_ref, vbn_ref, bias_ref, tri_ref,
                   *rest, t_new, n_heads, n_past_blocks):
    pps = PAGES_PER_STEP
    ka_pages, va_pages = rest[0:pps], rest[pps:2 * pps]
    kb_pages, vb_pages = rest[2 * pps:3 * pps], rest[3 * pps:4 * pps]
    oa_ref, ob_ref = rest[4 * pps], rest[4 * pps + 1]
    (qab_sc, qbb_sc, oblk_sc, gate_sc, max_sc, sum_sc, carry_sc, accb_sc) = rest[4 * pps + 2:]

    g = pl.program_id(1)
    rows = n_heads * t_new
    width = qa_ref.shape[2]
    blocks_per_step = pps // PAGES_PER_BLOCK
    own = n_past_blocks
    lane = lax.broadcasted_iota(jnp.int32, (rows, LANES), 1)
    r_t = lax.broadcasted_iota(jnp.int32, (rows, PAGE_SIZE), 0) % t_new
    key = lax.broadcasted_iota(jnp.int32, (rows, PAGE_SIZE), 1)

    def scores(q, k, token_major):
        k = k.astype(BF16)
        return _dot_nt(q, k) if token_major else _dot(q, k)

    def weighted_values(p, v, token_major):
        p, v = p.astype(BF16), v.astype(BF16)
        return _dot(p, v) if token_major else _dot_nt(p, v)

    def moba_chunk(k, v, j, bias, mask, token_major):
        s = scores(qab_sc[...], k, token_major)
        s = s[0:rows] + s[rows:2 * rows]
        gate = jnp.sum(s, axis=1, keepdims=True)
        s = s + bias
        if mask is not None:
            s = jnp.where(mask, s, -jnp.inf)
        m = jnp.max(s, axis=1, keepdims=True)
        p = jnp.exp(s - m)
        oblk_sc[j] = weighted_values(p, v, token_major)
        here = lane == j
        gate_sc[...] = jnp.where(here, gate, gate_sc[...])
        max_sc[...] = jnp.where(here, m, max_sc[...])
        sum_sc[...] = jnp.where(here, jnp.sum(p, axis=1, keepdims=True), sum_sc[...])

    def sb_chunk(k, v, carry, mask, token_major):
        log_beta, log_keep = _sb_terms(scores(qbb_sc[...], k, token_major))
        n = log_keep.shape[1]
        if mask is not None:
            log_keep = jnp.where(mask, log_keep, 0.0)
        incl = _suffix_sums(log_keep, tri_ref[0:n, 0:n])
        a = jnp.exp(log_beta + (incl - log_keep) + carry)
        if mask is not None:
            a = jnp.where(mask, a, 0.0)
        accb_sc[...] += weighted_values(a, v, token_major)
        return carry + incl[:, 0:1]

    @pl.when(g == 0)
    def _init():
        r_h = lax.broadcasted_iota(jnp.int32, (rows, width), 0) // t_new
        c_h = lax.broadcasted_iota(jnp.int32, (rows, width), 1) // HEAD_DIM
        own_head = r_h == c_h

        def expand(q):
            return jnp.where(own_head, jnp.concatenate([q] * n_heads, axis=0), 0.0)

        qa_hi, qa_lo = _split_bf16(expand(qa_ref[0]))
        qab_sc[0:rows] = qa_hi
        qab_sc[rows:2 * rows] = qa_lo
        qbb_sc[...] = expand(qb_ref[0]).astype(BF16)
        gate_sc[...] = jnp.zeros_like(gate_sc)
        max_sc[...] = jnp.zeros_like(max_sc)
        sum_sc[...] = jnp.zeros_like(sum_sc)
        accb_sc[...] = jnp.zeros_like(accb_sc)

        def pad(x):
            return jnp.concatenate([x, jnp.zeros((PAGE_SIZE - t_new, width), F32)], axis=0)

        moba_chunk(pad(kan_ref[0]), pad(van_ref[0]), own, bias_ref[2, :, 0:PAGE_SIZE], key <= r_t, True)
        carry_sc[...] = sb_chunk(pad(kbn_ref[0]), pad(vbn_ref[0]), jnp.zeros((rows, 1), F32),
                                 key < r_t, True)

    first_block = n_past_blocks - blocks_per_step * (g + 1)
    carry = carry_sc[...]
    for c in reversed(range(blocks_per_step)):
        pages = slice(c * PAGES_PER_BLOCK, (c + 1) * PAGES_PER_BLOCK)
        cat = lambda refs: jnp.concatenate([r[0] for r in refs[pages]], axis=1)
        j = first_block + c
        bias = bias_ref[jnp.where(j == n_past_blocks - 1, 1, 0)]
        moba_chunk(cat(ka_pages), cat(va_pages), j, bias, None, False)
        carry = sb_chunk(cat(kb_pages), cat(vb_pages), carry, None, False)
    carry_sc[...] = carry

    @pl.when(g == pl.num_programs(1) - 1)
    def _finish():
        sel = _top_blocks(gate_sc[...], lane < n_past_blocks, lane, n_past_blocks) | (lane == own)
        m_blk = max_sc[...]
        m_fin = jnp.max(jnp.where(sel, m_blk, -jnp.inf), axis=1, keepdims=True)
        wgt = jnp.where(sel, jnp.exp(m_blk - m_fin), 0.0)
        denom = jnp.sum(wgt * sum_sc[...], axis=1, keepdims=True)

        def combine(j, acc):
            return acc + _lane_pick(wgt, lane, j) * oblk_sc[j]

        o = lax.fori_loop(0, own + 1, combine, jnp.zeros((rows, width), F32)) / denom
        r_h = lax.broadcasted_iota(jnp.int32, (rows, width), 0) // t_new
        c_h = lax.broadcasted_iota(jnp.int32, (rows, width), 1) // HEAD_DIM
        own_head = r_h == c_h

        def fold(x):
            x = jnp.where(own_head, x, 0.0)
            out = x[0:t_new]
            for h in range(1, n_heads):
                out = out + x[h * t_new:(h + 1) * t_new]
            return out

        oa_ref[0] = fold(o)
        ob_ref[0] = fold(accb_sc[...])


def _sample_mixers(page_table, qa, qb, ka_new, va_new, kb_new, vb_new,
                   ck_a, cv_a, ck_b, cv_b, bias_tabs, tri):
    bd, t_new, width = qa.shape
    n_heads = width // HEAD_DIM
    n_pages = page_table.shape[1]
    n_past_blocks = n_pages // PAGES_PER_BLOCK
    steps = n_pages // PAGES_PER_STEP
    rows = n_heads * t_new
    assert n_past_blocks + 1 <= LANES and n_pages % PAGES_PER_STEP == 0

    tok_spec = pl.BlockSpec((1, t_new, width), lambda b, g, pt: (b, 0, 0))

    def page_spec(k):
        def index(b, g, pt):
            return (pt[b, n_pages - PAGES_PER_STEP * (g + 1) + k], 0, 0)
        return pl.BlockSpec((1, width, PAGE_SIZE), index)

    page_specs = [page_spec(k) for k in range(PAGES_PER_STEP)]
    const = lambda shape: pl.BlockSpec(shape, lambda b, g, pt: (0,) * len(shape),
                                       pipeline_mode=pl.Buffered(1))
    grid_spec = pltpu.PrefetchScalarGridSpec(
        num_scalar_prefetch=1,
        grid=(bd, steps),
        in_specs=[tok_spec] * 6 + [const(bias_tabs.shape), const(tri.shape)] + page_specs * 4,
        out_specs=[tok_spec, tok_spec],
        scratch_shapes=[pltpu.VMEM((2 * rows, width), BF16), pltpu.VMEM((rows, width), BF16),
                        pltpu.VMEM((n_past_blocks + 1, rows, width), F32),
                        pltpu.VMEM((rows, LANES), F32), pltpu.VMEM((rows, LANES), F32),
                        pltpu.VMEM((rows, LANES), F32), pltpu.VMEM((rows, 1), F32),
                        pltpu.VMEM((rows, width), F32)],
    )
    out = jax.ShapeDtypeStruct((bd, t_new, width), F32)
    pages = ([ck_a] * PAGES_PER_STEP + [cv_a] * PAGES_PER_STEP
             + [ck_b] * PAGES_PER_STEP + [cv_b] * PAGES_PER_STEP)
    return pl.pallas_call(
        functools.partial(_sample_kernel, t_new=t_new, n_heads=n_heads, n_past_blocks=n_past_blocks),
        grid_spec=grid_spec,
        out_shape=[out, out],
        compiler_params=_compiler_params(("parallel", "arbitrary")),
        name="sample_mixers",
    )(page_table, qa, qb, ka_new, va_new, kb_new, vb_new, bias_tabs, tri, *pages)


def _mlp_kernel(x_ref, oa_ref, ob_ref, ga_ref, gb_ref, wa_ref, wb_ref, wo_ref, ng_ref, wu_ref, wd_ref,
                y_ref):
    branch_a = _dot(oa_ref[...].astype(BF16), wa_ref[...])
    branch_b = _dot(ob_ref[...].astype(BF16), wb_ref[...])
    mixed = jax.nn.sigmoid(ga_ref[...]) * branch_a + jax.nn.sigmoid(gb_ref[...]) * branch_b
    x = x_ref[...] + _dot(mixed.astype(BF16), wo_ref[...])
    ms = jnp.mean(x * x, axis=-1, keepdims=True)
    h = (x * lax.rsqrt(ms + EPS) * ng_ref[...]).astype(BF16)
    up = jnp.maximum(_dot(h, wu_ref[...]), 0.0)
    y_ref[...] = x + _dot((up * up).astype(BF16), wd_ref[...])


def _merge_mlp(x, oa, ob, ga, gb, wa, wb, wo, ng, wu, wd):
    n, d_model = x.shape
    ts = TOKEN_TILE
    row = lambda w: pl.BlockSpec((ts, w), lambda i: (i, 0))
    return pl.pallas_call(
        _mlp_kernel,
        grid=(n // ts,),
        in_specs=[row(d_model), row(oa.shape[1]), row(ob.shape[1]), row(d_model), row(d_model),
                  _const_spec(wa.shape), _const_spec(wb.shape), _const_spec(wo.shape),
                  _const_spec(ng.shape), _const_spec(wu.shape), _const_spec(wd.shape)],
        out_specs=row(d_model),
        out_shape=jax.ShapeDtypeStruct((n, d_model), F32),
        compiler_params=_compiler_params(("parallel",)),
        name="merge_mlp",
    )(x, oa, ob, ga, gb, wa, wb, wo, ng, wu, wd)


def kernel(x_prompt, x_sample, cache_k_a, cache_v_a, cache_k_b, cache_v_b, page_table, rel_bias,
           norm_attn, w_in, q_norm, k_norm, w_br_a, w_br_b, w_o, norm_mlp, w_up, w_down):
    depth = w_in.shape[0]
    batch, seq, d_model = x_prompt.shape
    dec_batch, t_new, _ = x_sample.shape
    n_heads_a = cache_k_a.shape[3]
    n_heads_b = cache_k_b.shape[3]
    w_a, w_b = n_heads_a * HEAD_DIM, n_heads_b * HEAD_DIM
    assert w_a == w_b and n_heads_a == n_heads_b
    n_phys = cache_k_a.shape[1]
    past_len = page_table.shape[1] * PAGE_SIZE
    assert seq % MOBA_BLOCK == 0 and past_len % MOBA_BLOCK == 0 and t_new <= PAGE_SIZE
    assert TOKEN_TILE == MOBA_BLOCK

    relb_t = rel_bias.T.astype(F32)
    bias_prompt = _prompt_bias(relb_t)
    far_prompt = jnp.broadcast_to(relb_t[:, N_BUCKETS - 1][:, None, None], (n_heads_a, 1, MOBA_BLOCK))
    bias_sample = _sample_bias(relb_t.reshape(-1), n_heads_a, t_new, past_len)
    head_of = np.arange(w_a) // HEAD_DIM
    gmat = jnp.asarray((head_of[:, None] == head_of[None, :]) / HEAD_DIM, BF16)
    idx = np.arange(MOBA_BLOCK)
    tri = jnp.asarray(idx[:, None] >= idx[None, :], BF16)

    xp = x_prompt.reshape(batch * seq, d_model)
    xs = x_sample.reshape(dec_batch * t_new, d_model)
    prompt_kv, sample_kv = [], []
    for layer in range(depth):
        ng = norm_attn[layer][None]
        w_in_bf = w_in[layer].astype(BF16)
        qn_t = jnp.tile(q_norm[layer], n_heads_a)[None]
        kn_t = jnp.tile(k_norm[layer], n_heads_a)[None]
        mlp_w = (w_br_a[layer].astype(BF16), w_br_b[layer].astype(BF16), w_o[layer].astype(BF16),
                 norm_mlp[layer][None], w_up[layer].astype(BF16), w_down[layer].astype(BF16))

        (ka, va, kb, vb, ga, gb, qah, qal, kab, vab, qbb, kbb, vbb, km) = _proj(
            xp, ng, w_in_bf, gmat, qn_t, kn_t, w_a, w_b)
        seq3 = lambda t: t.reshape(batch, seq, t.shape[-1])
        oa = _moba_prompt(seq3(qah), seq3(qal), seq3(kab), seq3(vab),
                          km.reshape(batch, seq // MOBA_BLOCK, w_a), bias_prompt, far_prompt)
        ob = _sb_prompt(seq3(qbb), seq3(kbb), seq3(vbb), tri)
        xp = _merge_mlp(xp, oa.reshape(-1, w_a), ob.reshape(-1, w_b), ga, gb, *mlp_w)
        heads = lambda t, n: t.reshape(batch, seq, n, HEAD_DIM)
        prompt_kv.append((heads(ka, n_heads_a), heads(va, n_heads_a),
                          heads(kb, n_heads_b), heads(vb, n_heads_b)))

        (ka, va, kb, vb, ga, gb, qah, qal, _, _, qbb, _, _, _) = _proj(
            xs, ng, w_in_bf, gmat, qn_t, kn_t, w_a, w_b)
        tok3 = lambda t: t.reshape(dec_batch, t_new, t.shape[-1])
        qa_f32 = qah.astype(F32) + qal.astype(F32)
        pool = lambda c: jnp.transpose(c, (0, 1, 3, 4, 2)).reshape(depth * n_phys, -1, PAGE_SIZE)
        oa, ob = _sample_mixers(page_table + layer * n_phys,tok3(qa_f32), tok3(qbb.astype(F32)),
                                tok3(ka), tok3(va), tok3(kb), tok3(vb),
                                pool(cache_k_a), pool(cache_v_a), pool(cache_k_b), pool(cache_v_b),
                                bias_sample, tri)
        xs = _merge_mlp(xs, oa.reshape(-1, w_a), ob.reshape(-1, w_b), ga, gb, *mlp_w)
        heads = lambda t, n: t.reshape(dec_batch, t_new, n, HEAD_DIM)
        sample_kv.append((heads(ka, n_heads_a), heads(va, n_heads_a),
                          heads(kb, n_heads_b), heads(vb, n_heads_b)))

    stack = lambda kvs, i: jnp.stack([kv[i] for kv in kvs])
    return (xp.reshape(batch, seq, d_model), xs.reshape(dec_batch, t_new, d_model),
            *(stack(prompt_kv, i) for i in range(4)), *(stack(sample_kv, i) for i in range(4)))
```

```python
import functools
import math

import jax
import jax.numpy as jnp
import numpy as np
from jax import lax
from jax.experimental import pallas as pl
from jax.experimental.pallas import tpu as pltpu

F32 = jnp.float32
BF16 = jnp.bfloat16

HEAD_DIM = 64
MOBA_BLOCK = 256
MOBA_TOPK = 3
N_BUCKETS = 32
MAX_DISTANCE = 128
EPS = 1e-6
SCALE = HEAD_DIM ** -0.5
LOG2_E = math.log2(math.e)
PAGE_SIZE = 128

LANES = 128
HEADS_PER_TILE = LANES // HEAD_DIM
TOKEN_TILE = 256
QUERY_TILE = 512
NEG_SCORE = -1e30
MAX_INIT = -1e29
PAGES_PER_STEP = 8
PAGES_PER_BLOCK = MOBA_BLOCK // PAGE_SIZE
VMEM_LIMIT = 56 * 1024 * 1024

NT_DIMS = (((1,), (1,)), ((), ()))


def _dot(a, b):
    return jnp.dot(a, b, preferred_element_type=F32)


def _dot_nt(a, b):
    return lax.dot_general(a, b, NT_DIMS, preferred_element_type=F32)


def _split_bf16(x):
    hi = x.astype(BF16)
    lo = (x - hi.astype(F32)).astype(BF16)
    return hi, lo


def _t5_bucket_np(dist):
    max_exact = N_BUCKETS // 2
    d = np.maximum(dist, 0)
    df = np.maximum(d, 1).astype(np.float32)
    large = max_exact + (np.log(df / max_exact) / math.log(MAX_DISTANCE / max_exact)
                         * (N_BUCKETS - max_exact)).astype(np.int32)
    large = np.minimum(large, N_BUCKETS - 1)
    return np.where(d < max_exact, d, large).astype(np.int32)


def _compiler_params(semantics):
    return pltpu.CompilerParams(dimension_semantics=semantics, vmem_limit_bytes=VMEM_LIMIT)


def _const_spec(shape):
    zeros = (0,) * len(shape)
    return pl.BlockSpec(shape, lambda *_: zeros, pipeline_mode=pl.Buffered(1))


def _proj_kernel(x_ref, ng_ref, w_ref, gmat_ref, qn_ref, kn_ref,
                 ka_ref, va_ref, kb_ref, vb_ref, ga_ref, gb_ref,
                 qah_ref, qal_ref, kab_ref, vab_ref, qbb_ref, kbb_ref, vbb_ref, km_ref,
                 *, w_a, w_b, d_model):
    x = x_ref[...]
    ms = jnp.mean(x * x, axis=-1, keepdims=True)
    hb = (x * lax.rsqrt(ms + EPS) * ng_ref[...]).astype(BF16)

    def seg(c0, n):
        return _dot(hb, w_ref[:, c0:c0 + n])

    def head_norm(t, g_ref):
        hi, lo = _split_bf16(t * t)
        gm = gmat_ref[...]
        msq = _dot(hi, gm) + _dot(lo, gm)
        return t * lax.rsqrt(msq + EPS) * g_ref[...]

    c = 0
    qa = head_norm(seg(c, w_a), qn_ref); c += w_a
    qa = qa * SCALE
    qah, qal = _split_bf16(qa)
    qah_ref[...] = qah
    qal_ref[...] = qal
    ka = head_norm(seg(c, w_a), kn_ref); c += w_a
    ka_ref[...] = ka
    kab_ref[...] = ka.astype(BF16)
    km_ref[0] = jnp.mean(ka, axis=0, keepdims=True)
    va = seg(c, w_a); c += w_a
    va_ref[...] = va
    vab_ref[...] = va.astype(BF16)
    qb = seg(c, w_b); c += w_b
    qbb_ref[...] = (qb * (SCALE * LOG2_E)).astype(BF16)
    kb = seg(c, w_b); c += w_b
    kb_ref[...] = kb
    kbb_ref[...] = kb.astype(BF16)
    vb = seg(c, w_b); c += w_b
    vb_ref[...] = vb
    vbb_ref[...] = vb.astype(BF16)
    ga_ref[...] = seg(c, d_model); c += d_model
    gb_ref[...] = seg(c, d_model)


def _proj(x, norm_g, w_in_bf, gmat, qn_t, kn_t, w_a, w_b):
    n, d_model = x.shape
    ts = TOKEN_TILE
    steps = n // ts
    row = lambda w: pl.BlockSpec((ts, w), lambda i: (i, 0))
    f32 = lambda w: jax.ShapeDtypeStruct((n, w), F32)
    bf = lambda w: jax.ShapeDtypeStruct((n, w), BF16)
    out_shape = ([f32(w_a), f32(w_a), f32(w_b), f32(w_b), f32(d_model), f32(d_model)]
                 + [bf(w_a)] * 4 + [bf(w_b)] * 3
                 + [jax.ShapeDtypeStruct((steps, 1, w_a), F32)])
    out_specs = ([row(w_a), row(w_a), row(w_b), row(w_b), row(d_model), row(d_model)]
                 + [row(w_a)] * 4 + [row(w_b)] * 3
                 + [pl.BlockSpec((1, 1, w_a), lambda i: (i, 0, 0))])
    return pl.pallas_call(
        functools.partial(_proj_kernel, w_a=w_a, w_b=w_b, d_model=d_model),
        grid=(steps,),
        in_specs=[row(d_model), _const_spec((1, d_model)), _const_spec(w_in_bf.shape),
                  _const_spec(gmat.shape), _const_spec((1, w_a)), _const_spec((1, w_a))],
        out_specs=out_specs,
        out_shape=out_shape,
        compiler_params=_compiler_params(("parallel",)),
        name="proj",
    )(x, norm_g, w_in_bf, gmat, qn_t, kn_t)


def _prompt_bias_kernel(relb_ref, bkt_ref, out_ref):
    h = pl.program_id(0)
    b = bkt_ref[...]
    acc = jnp.zeros(b.shape, F32)
    for n in range(N_BUCKETS):
        acc = jnp.where(b == n, relb_ref[h, n], acc)
    out_ref[0] = acc


def _prompt_bias(relb_t):
    n_heads = relb_t.shape[0]
    r = np.arange(MOBA_BLOCK)[:, None]
    c = np.arange(MOBA_BLOCK)[None, :]
    bkt = np.stack([_t5_bucket_np(r - c), _t5_bucket_np(MOBA_BLOCK + r - c)])
    return pl.pallas_call(
        _prompt_bias_kernel,
        grid=(n_heads,),
        in_specs=[pl.BlockSpec(memory_space=pltpu.SMEM), _const_spec(bkt.shape)],
        out_specs=pl.BlockSpec((1,) + bkt.shape, lambda h: (h, 0, 0, 0)),
        out_shape=jax.ShapeDtypeStruct((n_heads,) + bkt.shape, F32),
        compiler_params=_compiler_params(("parallel",)),
        name="prompt_bias",
    )(relb_t, jnp.asarray(bkt))


def _sample_bias_kernel(relb_ref, idx_ref, out_ref, *, n_entries):
    idx = idx_ref[...]
    acc = jnp.zeros(idx.shape, F32)
    for n in range(n_entries):
        acc = jnp.where(idx == n, relb_ref[n], acc)
    out_ref[...] = acc


def _sample_bias(relb_flat, n_heads, t_new, past_len):
    rows = n_heads * t_new
    head = (np.arange(rows) // t_new)[:, None]
    t = (np.arange(rows) % t_new)[:, None]
    key = np.arange(MOBA_BLOCK)[None, :]
    far = np.full((rows, MOBA_BLOCK), N_BUCKETS - 1, np.int32)
    last = _t5_bucket_np(past_len + t - (past_len - MOBA_BLOCK + key))
    new = _t5_bucket_np(t - key)
    idx = (np.stack([far, last, new]) + head[None] * N_BUCKETS).astype(np.int32)
    return pl.pallas_call(
        functools.partial(_sample_bias_kernel, n_entries=n_heads * N_BUCKETS),
        in_specs=[pl.BlockSpec(memory_space=pltpu.SMEM), pl.BlockSpec(memory_space=pltpu.VMEM)],
        out_specs=pl.BlockSpec(memory_space=pltpu.VMEM),
        out_shape=jax.ShapeDtypeStruct(idx.shape, F32),
        name="sample_bias",
    )(relb_flat, jnp.asarray(idx))


def _top_blocks(gate, valid, lane, n_cols):
    gate = jnp.where(valid, gate, -jnp.inf)
    rank = jnp.zeros(gate.shape, jnp.int32)
    for j in range(n_cols):
        cj = gate[:, j:j + 1]
        beats = (cj > gate) | ((cj == gate) & (lane > j))
        rank = rank + beats.astype(jnp.int32)
    return valid & (rank < MOBA_TOPK)


def _moba_kernel(qh_ref, ql_ref, k_ref, v_ref, km_ref, bias_ref, far_ref, o_ref):
    i = pl.program_id(2)
    tq = qh_ref.shape[1]
    halves = tq // MOBA_BLOCK
    n_blocks = km_ref.shape[1]
    void_lane = n_blocks
    assert n_blocks % 8 == 0 and n_blocks + 8 <= HEAD_DIM
    first = i * halves
    lane = lax.broadcasted_iota(jnp.int32, (1, LANES), 1)
    row = lax.broadcasted_iota(jnp.int32, (tq, MOBA_BLOCK), 0)
    col = lax.broadcasted_iota(jnp.int32, (tq, MOBA_BLOCK), 1)
    blk = lax.broadcasted_iota(jnp.int32, (n_blocks, tq), 0)
    own = first + lax.broadcasted_iota(jnp.int32, (n_blocks, tq), 1) // MOBA_BLOCK
    km_hi, km_lo = _split_bf16(km_ref[0])
    mine = [(lane // HEAD_DIM) == hs for hs in range(HEADS_PER_TILE)]
    other = [(1 - hs) * HEAD_DIM for hs in range(HEADS_PER_TILE)]

    def query_with_terms(hs):
        qh = jnp.where(mine[hs], qh_ref[0], 0)
        ql = jnp.where(mine[hs], ql_ref[0], 0)
        gate = _dot_nt(km_hi, qh) + _dot_nt(km_hi, ql) + _dot_nt(km_lo, qh)
        valid = blk < own
        gate = jnp.where(valid, gate, -jnp.inf)
        rank = jnp.zeros(gate.shape, jnp.int32)
        for j in range(n_blocks):
            gj = gate[j:j + 1, :]
            beats = (gj > gate) | ((gj == gate) & (blk > j))
            rank = rank + beats.astype(jnp.int32)
        keep = (valid & (rank < MOBA_TOPK)) | (blk == own)
        term = jnp.where(keep, 0.0, NEG_SCORE)
        pad = lambda n, value: [jnp.full((n, tq), value, F32)] if n else []
        term = jnp.concatenate(pad(other[hs], 0.0) + [term] + pad(8, NEG_SCORE)
                               + pad(LANES - other[hs] - n_blocks - 8, 0.0), axis=0)
        return jnp.where(mine[hs], qh_ref[0], term.T.astype(BF16))

    def attend(state, q, hs, j, bias, mask, void):
        m, acc = state
        rows = pl.ds(pl.multiple_of(jnp.maximum(j, 0) * MOBA_BLOCK, MOBA_BLOCK), MOBA_BLOCK)
        hot = (lane == other[hs] + j) | (lane == jnp.where(void, other[hs] + void_lane, -1))
        kj = jnp.where(mine[hs], k_ref[0, rows, :], jnp.where(hot, 1.0, 0.0).astype(BF16))
        vj = jnp.where(mine[hs], v_ref[0, rows, :], jnp.ones((1, LANES), BF16))
        s = _dot_nt(q, kj) + bias
        if mask is not None:
            s = jnp.where(mask, s, -jnp.inf)
        m_new = jnp.maximum(m, jnp.max(s, axis=1, keepdims=True))
        pv = _dot(jnp.exp(s - m_new).astype(BF16), vj)
        return m_new, jnp.exp(m - m_new) * acc + pv

    def band_bias(hs, h):
        def part(r):
            if r == h:
                return bias_ref[hs, 0]
            if r == h + 1:
                return bias_ref[hs, 1]
            return jnp.broadcast_to(far_ref[hs], (MOBA_BLOCK, MOBA_BLOCK))
        return jnp.concatenate([part(r) for r in range(halves)], axis=0)

    qs = [query_with_terms(hs) for hs in range(HEADS_PER_TILE)]
    states = [(jnp.full((tq, 1), MAX_INIT, F32), jnp.zeros((tq, LANES), F32))
              for _ in range(HEADS_PER_TILE)]
    for h in range(halves):
        causal = (row // MOBA_BLOCK != h) | (col <= row - h * MOBA_BLOCK)
        states = [attend(states[hs], qs[hs], hs, first + h, band_bias(hs, h), causal, False)
                  for hs in range(HEADS_PER_TILE)]
    states = [attend(states[hs], qs[hs], hs, first - 1, band_bias(hs, -1), None, i == 0)
              for hs in range(HEADS_PER_TILE)]

    def far_body(j, flat):
        out = []
        for hs in range(HEADS_PER_TILE):
            out += attend((flat[2 * hs], flat[2 * hs + 1]), qs[hs], hs, j, far_ref[hs], None, False)
        return tuple(out)

    flat = lax.fori_loop(0, jnp.maximum(first - 1, 0), far_body,
                         tuple(x for st in states for x in st))
    outs = [flat[2 * hs + 1] / flat[2 * hs + 1][:, other[hs]:other[hs] + 1]
            for hs in range(HEADS_PER_TILE)]
    o_ref[0] = jnp.where(lane < HEAD_DIM, outs[0], outs[1]).astype(o_ref.dtype)


def _moba_prompt(qh, ql, kb, vb, km, bias, far):
    b, s, w = qh.shape
    n_tiles = w // LANES
    nq = s // QUERY_TILE
    q_spec = pl.BlockSpec((1, QUERY_TILE, LANES), lambda bi, hp, i: (bi, i, hp))
    kv_spec = pl.BlockSpec((1, s, LANES), lambda bi, hp, i: (bi, 0, hp))
    return pl.pallas_call(
        _moba_kernel,
        grid=(b, n_tiles, nq),
        in_specs=[q_spec, q_spec, kv_spec, kv_spec,
                  pl.BlockSpec((1, km.shape[1], LANES), lambda bi, hp, i: (bi, 0, hp)),
                  pl.BlockSpec((HEADS_PER_TILE, 2, MOBA_BLOCK, MOBA_BLOCK),
                               lambda bi, hp, i: (hp, 0, 0, 0)),
                  pl.BlockSpec((HEADS_PER_TILE, 1, MOBA_BLOCK), lambda bi, hp, i: (hp, 0, 0))],
        out_specs=q_spec,
        out_shape=jax.ShapeDtypeStruct((b, s, w), BF16),
        compiler_params=_compiler_params(("parallel", "parallel", "arbitrary")),
        name="moba_prompt",
    )(qh, ql, kb, vb, km, bias, far)


def _softplus2(z2):
    return jnp.maximum(z2, 0.0) + jnp.log2(1.0 + jnp.exp2(-jnp.abs(z2)))


def _suffix_sums(x, tri2):
    hi, lo = _split_bf16(x)
    return _dot(jnp.concatenate([hi, lo], axis=1), tri2)


def _sb_weights(z2, incl, carry, n_sub):
    rows, sub = z2.shape[0], incl.shape[1]
    out = [None] * n_sub
    for c in reversed(range(n_sub)):
        inc = incl[c * rows:(c + 1) * rows]
        out[c] = jnp.exp2(z2[:, c * sub:(c + 1) * sub] - inc + carry)
        carry = carry - inc[:, 0:1]
    return jnp.concatenate(out, axis=1), carry


def _sb_kernel(q_ref, k_ref, v_ref, tri_ref, o_ref):
    i = pl.program_id(2)
    tq = q_ref.shape[1]
    sub = tri_ref.shape[1]
    n_sub = tq // sub
    lane = lax.broadcasted_iota(jnp.int32, (1, LANES), 1)
    row = lax.broadcasted_iota(jnp.int32, (tq, tq), 0)
    col = lax.broadcasted_iota(jnp.int32, (tq, tq), 1)

    def tile(state, qh, j, past):
        carry, acc = state
        rows = pl.ds(pl.multiple_of(j * tq, tq), tq)
        z2 = _dot_nt(qh, k_ref[0, rows, :])
        sp = _softplus2(z2)
        if past is not None:
            sp = jnp.where(past, sp, 0.0)
        stacked = jnp.concatenate([sp[:, c * sub:(c + 1) * sub] for c in range(n_sub)], axis=0)
        a, carry = _sb_weights(z2, _suffix_sums(stacked, tri_ref[...]), carry, n_sub)
        if past is not None:
            a = jnp.where(past, a, 0.0)
        return carry, acc + _dot(a.astype(BF16), v_ref[0, rows, :])

    heads = range(HEADS_PER_TILE)
    qs = [jnp.where((lane // HEAD_DIM) == hs, q_ref[0], 0) for hs in heads]
    states = [tile((jnp.zeros((tq, 1), F32), jnp.zeros((tq, LANES), F32)), qs[hs], i, col < row)
              for hs in heads]

    def body(t, flat):
        out = []
        for hs in heads:
            out += tile((flat[2 * hs], flat[2 * hs + 1]), qs[hs], i - 1 - t, None)
        return tuple(out)

    flat = lax.fori_loop(0, i, body, tuple(x for st in states for x in st))
    o_ref[0] = jnp.where(lane < HEAD_DIM, flat[1], flat[3]).astype(o_ref.dtype)


def _sb_prompt(q, kb, vb, tri):
    b, s, w = q.shape
    n_tiles = w // LANES
    tq = QUERY_TILE
    assert tq % tri.shape[1] == 0 and HEADS_PER_TILE == 2
    q_spec = pl.BlockSpec((1, tq, LANES), lambda bi, hp, i: (bi, i, hp))
    kv_spec = pl.BlockSpec((1, s, LANES), lambda bi, hp, i: (bi, 0, hp))
    return pl.pallas_call(
        _sb_kernel,
        grid=(b, n_tiles, s // tq),
        in_specs=[q_spec, kv_spec, kv_spec, _const_spec(tri.shape)],
        out_specs=q_spec,
        out_shape=jax.ShapeDtypeStruct((b, s, w), BF16),
        compiler_params=_compiler_params(("parallel", "parallel", "arbitrary")),
        name="sb_prompt",
    )(q, kb, vb, tri)


def _sample_kernel(pt_ref, qa_ref, qb_ref, kan_ref, van_ref, kbn_ref, vbn_ref, bias_ref, tri_ref,
                   *rest, t_new, n_heads, n_past_blocks):
    pps = PAGES_PER_STEP
    ka_pages, va_pages = rest[0:pps], rest[pps:2 * pps]
    kb_pages, vb_pages = rest[2 * pps:3 * pps], rest[3 * pps:4 * pps]
    oa_ref, ob_ref = rest[4 * pps], rest[4 * pps + 1]
    (qab_sc, qbb_sc, oblk_sc, gate_sc, max_sc, sum_sc, carry_sc, accb_sc) = rest[4 * pps + 2:]

    g = pl.program_id(1)
    rows = n_heads * t_new
    width = qa_ref.shape[2]
    n_blk = pps // PAGES_PER_BLOCK
    own = n_past_blocks
    lane = lax.broadcasted_iota(jnp.int32, (rows, LANES), 1)

    def record(entries):
        for ref, pos in ((gate_sc, 1), (max_sc, 2), (sum_sc, 3)):
            new = ref[...]
            for e in entries:
                new = jnp.where(lane == e[0], e[pos], new)
            ref[...] = new

    def softmax_partial(s, bias, mask):
        s = s[0:rows] + s[rows:2 * rows]
        gate = jnp.sum(s, axis=1, keepdims=True)
        s = s + bias
        if mask is not None:
            s = jnp.where(mask, s, -jnp.inf)
        m = jnp.max(s, axis=1, keepdims=True)
        p = jnp.exp(s - m)
        return p.astype(BF16), gate, m, jnp.sum(p, axis=1, keepdims=True)

    @pl.when(g == 0)
    def _init():
        r_h = lax.broadcasted_iota(jnp.int32, (rows, width), 0) // t_new
        c_h = lax.broadcasted_iota(jnp.int32, (rows, width), 1) // HEAD_DIM
        own_head = r_h == c_h

        def expand(q):
            return jnp.where(own_head, jnp.concatenate([q] * n_heads, axis=0), 0.0)

        qa_hi, qa_lo = _split_bf16(expand(qa_ref[0]))
        qab_sc[0:rows] = qa_hi
        qab_sc[rows:2 * rows] = qa_lo
        qb = expand(qb_ref[0]).astype(BF16)
        qbb_sc[...] = qb
        for ref in (gate_sc, max_sc, sum_sc):
            ref[...] = jnp.zeros_like(ref)

        def page_of(x):
            return jnp.concatenate([x, jnp.zeros((PAGE_SIZE - t_new, width), F32)], axis=0).astype(BF16)

        r_t = lax.broadcasted_iota(jnp.int32, (rows, PAGE_SIZE), 0) % t_new
        key = lax.broadcasted_iota(jnp.int32, (rows, PAGE_SIZE), 1)
        p, gate, m, l = softmax_partial(_dot_nt(qab_sc[...], page_of(kan_ref[0])),
                                        bias_ref[2, :, 0:PAGE_SIZE], key <= r_t)
        oblk_sc[own] = _dot(p, page_of(van_ref[0]))
        record([(own, gate, m, l)])

        past = key < r_t
        z = _dot_nt(qb, page_of(kbn_ref[0]))
        hi, lo = _split_bf16(jnp.where(past, _softplus2(z), 0.0))
        tri_page = tri_ref[0:PAGE_SIZE, 0:PAGE_SIZE]
        incl = _dot(hi, tri_page) + _dot(lo, tri_page)
        a = jnp.where(past, jnp.exp2(z - incl), 0.0)
        accb_sc[...] = _dot(a.astype(BF16), page_of(vbn_ref[0]))
        carry_sc[...] = -incl[:, 0:1]

    first_block = n_past_blocks - n_blk * (g + 1)
    cat = lambda refs: jnp.concatenate([r[0] for r in refs], axis=1).astype(BF16)
    keys_of = lambda c: slice(c * MOBA_BLOCK, (c + 1) * MOBA_BLOCK)
    s_all = _dot(qab_sc[...], cat(ka_pages))
    z = _dot(qbb_sc[...], cat(kb_pages))

    sp = _softplus2(z)
    stacked = jnp.concatenate([sp[:, keys_of(c)] for c in range(n_blk)], axis=0)
    a, carry = _sb_weights(z, _suffix_sums(stacked, tri_ref[...]), carry_sc[...], n_blk)
    carry_sc[...] = carry
    accb_sc[...] += _dot_nt(a.astype(BF16), cat(vb_pages))

    va = cat(va_pages)
    entries = []
    for c in range(n_blk):
        j = first_block + c
        bias = bias_ref[jnp.where(j == n_past_blocks - 1, 1, 0)]
        p, gate, m, l = softmax_partial(s_all[:, keys_of(c)], bias, None)
        oblk_sc[j] = _dot_nt(p, va[:, keys_of(c)])
        entries.append((j, gate, m, l))
    record(entries)

    @pl.when(g == pl.num_programs(1) - 1)
    def _finish():
        sel = _top_blocks(gate_sc[...], lane < n_past_blocks, lane, n_past_blocks) | (lane == own)
        m_blk = max_sc[...]
        m_fin = jnp.max(jnp.where(sel, m_blk, -jnp.inf), axis=1, keepdims=True)
        wgt = jnp.where(sel, jnp.exp(m_blk - m_fin), 0.0)
        denom = jnp.sum(wgt * sum_sc[...], axis=1, keepdims=True)
        o = wgt[:, 0:1] * oblk_sc[0]
        for j in range(1, own + 1):
            o = o + wgt[:, j:j + 1] * oblk_sc[j]
        o = o / denom
        r_h = lax.broadcasted_iota(jnp.int32, (rows, width), 0) // t_new
        c_h = lax.broadcasted_iota(jnp.int32, (rows, width), 1) // HEAD_DIM
        own_head = r_h == c_h

        def fold(x):
            x = jnp.where(own_head, x, 0.0)
            out = x[0:t_new]
            for h in range(1, n_heads):
                out = out + x[h * t_new:(h + 1) * t_new]
            return out

        oa_ref[0] = fold(o)
        ob_ref[0] = fold(accb_sc[...])


def _sample_mixers(page_table, qa, qb, ka_new, va_new, kb_new, vb_new,
                   ck_a, cv_a, ck_b, cv_b, bias_tabs, tri):
    bd, t_new, width = qa.shape
    n_heads = width // HEAD_DIM
    n_pages = page_table.shape[1]
    n_past_blocks = n_pages // PAGES_PER_BLOCK
    steps = n_pages // PAGES_PER_STEP
    rows = n_heads * t_new
    assert n_past_blocks + 1 <= LANES and n_pages % PAGES_PER_STEP == 0

    tok_spec = pl.BlockSpec((1, t_new, width), lambda b, g, pt: (b, 0, 0))

    def page_spec(k):
        def index(b, g, pt):
            return (pt[b, n_pages - PAGES_PER_STEP * (g + 1) + k], 0, 0)
        return pl.BlockSpec((1, width, PAGE_SIZE), index)

    page_specs = [page_spec(k) for k in range(PAGES_PER_STEP)]
    const = lambda shape: pl.BlockSpec(shape, lambda b, g, pt: (0,) * len(shape),
                                       pipeline_mode=pl.Buffered(1))
    grid_spec = pltpu.PrefetchScalarGridSpec(
        num_scalar_prefetch=1,
        grid=(bd, steps),
        in_specs=[tok_spec] * 6 + [const(bias_tabs.shape), const(tri.shape)] + page_specs * 4,
        out_specs=[tok_spec, tok_spec],
        scratch_shapes=[pltpu.VMEM((2 * rows, width), BF16), pltpu.VMEM((rows, width), BF16),
                        pltpu.VMEM((n_past_blocks + 1, rows, width), F32),
                        pltpu.VMEM((rows, LANES), F32), pltpu.VMEM((rows, LANES), F32),
                        pltpu.VMEM((rows, LANES), F32), pltpu.VMEM((rows, 1), F32),
                        pltpu.VMEM((rows, width), F32)],
    )
    out = jax.ShapeDtypeStruct((bd, t_new, width), F32)
    pages = ([ck_a] * PAGES_PER_STEP + [cv_a] * PAGES_PER_STEP
             + [ck_b] * PAGES_PER_STEP + [cv_b] * PAGES_PER_STEP)
    return pl.pallas_call(
        functools.partial(_sample_kernel, t_new=t_new, n_heads=n_heads, n_past_blocks=n_past_blocks),
        grid_spec=grid_spec,
        out_shape=[out, out],
        compiler_params=_compiler_params(("parallel", "arbitrary")),
        name="sample_mixers",
    )(page_table, qa, qb, ka_new, va_new, kb_new, vb_new, bias_tabs, tri, *pages)


def _mlp_kernel(x_ref, oa_ref, ob_ref, ga_ref, gb_ref, wa_ref, wb_ref, wo_ref, ng_ref, wu_ref, wd_ref,
                y_ref):
    branch_a = _dot(oa_ref[...].astype(BF16), wa_ref[...])
    branch_b = _dot(ob_ref[...].astype(BF16), wb_ref[...])
    mixed = jax.nn.sigmoid(ga_ref[...]) * branch_a + jax.nn.sigmoid(gb_ref[...]) * branch_b
    x = x_ref[...] + _dot(mixed.astype(BF16), wo_ref[...])
    ms = jnp.mean(x * x, axis=-1, keepdims=True)
    h = (x * lax.rsqrt(ms + EPS) * ng_ref[...]).astype(BF16)
    up = jnp.maximum(_dot(h, wu_ref[...]), 0.0)
    y_ref[...] = x + _dot((up * up).astype(BF16), wd_ref[...])


def _merge_mlp(x, oa, ob, ga, gb, wa, wb, wo, ng, wu, wd):
    n, d_model = x.shape
    ts = TOKEN_TILE
    row = lambda w: pl.BlockSpec((ts, w), lambda i: (i, 0))
    return pl.pallas_call(
        _mlp_kernel,
        grid=(n // ts,),
        in_specs=[row(d_model), row(oa.shape[1]), row(ob.shape[1]), row(d_model), row(d_model),
                  _const_spec(wa.shape), _const_spec(wb.shape), _const_spec(wo.shape),
                  _const_spec(ng.shape), _const_spec(wu.shape), _const_spec(wd.shape)],
        out_specs=row(d_model),
        out_shape=jax.ShapeDtypeStruct((n, d_model), F32),
        compiler_params=_compiler_params(("parallel",)),
        name="merge_mlp",
    )(x, oa, ob, ga, gb, wa, wb, wo, ng, wu, wd)


def kernel(x_prompt, x_sample, cache_k_a, cache_v_a, cache_k_b, cache_v_b, page_table, rel_bias,
           norm_attn, w_in, q_norm, k_norm, w_br_a, w_br_b, w_o, norm_mlp, w_up, w_down):
    depth = w_in.shape[0]
    batch, seq, d_model = x_prompt.shape
    dec_batch, t_new, _ = x_sample.shape
    n_heads_a = cache_k_a.shape[3]
    n_heads_b = cache_k_b.shape[3]
    w_a, w_b = n_heads_a * HEAD_DIM, n_heads_b * HEAD_DIM
    assert w_a == w_b and n_heads_a == n_heads_b
    n_phys = cache_k_a.shape[1]
    past_len = page_table.shape[1] * PAGE_SIZE
    assert seq % MOBA_BLOCK == 0 and past_len % MOBA_BLOCK == 0 and t_new <= PAGE_SIZE
    assert TOKEN_TILE == MOBA_BLOCK and seq % QUERY_TILE == 0

    relb_t = rel_bias.T.astype(F32)
    bias_prompt = _prompt_bias(relb_t)
    far_prompt = jnp.broadcast_to(relb_t[:, N_BUCKETS - 1][:, None, None], (n_heads_a, 1, MOBA_BLOCK))
    bias_sample = _sample_bias(relb_t.reshape(-1), n_heads_a, t_new, past_len)
    head_of = np.arange(w_a) // HEAD_DIM
    gmat = jnp.asarray((head_of[:, None] == head_of[None, :]) / HEAD_DIM, BF16)
    idx = np.arange(MOBA_BLOCK)
    tri = idx[:, None] >= idx[None, :]
    tri = jnp.asarray(np.concatenate([tri, tri], axis=0), BF16)

    xp = x_prompt.reshape(batch * seq, d_model)
    xs = x_sample.reshape(dec_batch * t_new, d_model)
    prompt_kv, sample_kv = [], []
    for layer in range(depth):
        ng = norm_attn[layer][None]
        w_in_bf = w_in[layer].astype(BF16)
        qn_t = jnp.tile(q_norm[layer], n_heads_a)[None]
        kn_t = jnp.tile(k_norm[layer], n_heads_a)[None]
        mlp_w = (w_br_a[layer].astype(BF16), w_br_b[layer].astype(BF16), w_o[layer].astype(BF16),
                 norm_mlp[layer][None], w_up[layer].astype(BF16), w_down[layer].astype(BF16))

        (ka, va, kb, vb, ga, gb, qah, qal, kab, vab, qbb, kbb, vbb, km) = _proj(
            xp, ng, w_in_bf, gmat, qn_t, kn_t, w_a, w_b)
        seq3 = lambda t: t.reshape(batch, seq, t.shape[-1])
        oa = _moba_prompt(seq3(qah), seq3(qal), seq3(kab), seq3(vab),
                          km.reshape(batch, seq // MOBA_BLOCK, w_a), bias_prompt, far_prompt)
        ob = _sb_prompt(seq3(qbb), seq3(kbb), seq3(vbb), tri)
        xp = _merge_mlp(xp, oa.reshape(-1, w_a), ob.reshape(-1, w_b), ga, gb, *mlp_w)
        heads = lambda t, n: t.reshape(batch, seq, n, HEAD_DIM)
        prompt_kv.append((heads(ka, n_heads_a), heads(va, n_heads_a),
                          heads(kb, n_heads_b), heads(vb, n_heads_b)))

        (ka, va, kb, vb, ga, gb, qah, qal, _, _, qbb, _, _, _) = _proj(
            xs, ng, w_in_bf, gmat, qn_t, kn_t, w_a, w_b)
        tok3 = lambda t: t.reshape(dec_batch, t_new, t.shape[-1])
        qa_f32 = qah.astype(F32) + qal.astype(F32)
        pool = lambda c: jnp.transpose(c, (0, 1, 3, 4, 2)).reshape(depth * n_phys, -1, PAGE_SIZE)
        oa, ob = _sample_mixers(page_table + layer * n_phys, tok3(qa_f32), tok3(qbb.astype(F32)),
                                tok3(ka), tok3(va), tok3(kb), tok3(vb),
                                pool(cache_k_a), pool(cache_v_a), pool(cache_k_b), pool(cache_v_b),
                                bias_sample, tri)
        xs = _merge_mlp(xs, oa.reshape(-1, w_a), ob.reshape(-1, w_b), ga, gb, *mlp_w)
        heads = lambda t, n: t.reshape(dec_batch, t_new, n, HEAD_DIM)
        sample_kv.append((heads(ka, n_heads_a), heads(va, n_heads_a),
                          heads(kb, n_heads_b), heads(vb, n_heads_b)))

    stack = lambda kvs, i: jnp.stack([kv[i] for kv in kvs])
    return (xp.reshape(batch, seq, d_model), xs.reshape(dec_batch, t_new, d_model),
            *(stack(prompt_kv, i) for i in range(4)), *(stack(sample_kv, i) for i in range(4)))
```

```python
import functools
import math

import jax
import jax.numpy as jnp
import numpy as np
from jax import lax
from jax.experimental import pallas as pl
from jax.experimental.pallas import tpu as pltpu

F32 = jnp.float32
BF16 = jnp.bfloat16

HEAD_DIM = 64
MOBA_BLOCK = 256
MOBA_TOPK = 3
N_BUCKETS = 32
MAX_DISTANCE = 128
EPS = 1e-6
SCALE = HEAD_DIM ** -0.5
LOG2_E = math.log2(math.e)
PAGE_SIZE = 128

LANES = 128
HEADS_PER_TILE = LANES // HEAD_DIM
TOKEN_TILE = 256
QUERY_TILE = 512
NEG_SCORE = -1e30
MAX_INIT = -1e29
F32_EXP2_UNDERFLOW = -160.0
PAGES_PER_STEP = 8
PAGES_PER_BLOCK = MOBA_BLOCK // PAGE_SIZE
VMEM_LIMIT = 56 * 1024 * 1024

NT_DIMS = (((1,), (1,)), ((), ()))


def _dot(a, b):
    return jnp.dot(a, b, preferred_element_type=F32)


def _dot_nt(a, b):
    return lax.dot_general(a, b, NT_DIMS, preferred_element_type=F32)


def _split_bf16(x):
    hi = x.astype(BF16)
    lo = (x - hi.astype(F32)).astype(BF16)
    return hi, lo


def _t5_bucket_np(dist):
    max_exact = N_BUCKETS // 2
    d = np.maximum(dist, 0)
    df = np.maximum(d, 1).astype(np.float32)
    large = max_exact + (np.log(df / max_exact) / math.log(MAX_DISTANCE / max_exact)
                         * (N_BUCKETS - max_exact)).astype(np.int32)
    large = np.minimum(large, N_BUCKETS - 1)
    return np.where(d < max_exact, d, large).astype(np.int32)


def _compiler_params(semantics):
    return pltpu.CompilerParams(dimension_semantics=semantics, vmem_limit_bytes=VMEM_LIMIT)


def _const_spec(shape):
    zeros = (0,) * len(shape)
    return pl.BlockSpec(shape, lambda *_: zeros, pipeline_mode=pl.Buffered(1))


def _proj_kernel(x_ref, ng_ref, w_ref, gmat_ref, qn_ref, kn_ref,
                 ka_ref, va_ref, kb_ref, vb_ref, ga_ref, gb_ref,
                 qah_ref, qal_ref, kab_ref, vab_ref, qbb_ref, kbb_ref, vbb_ref, km_ref,
                 *, w_a, w_b, d_model, kv_by_column):
    x = x_ref[...]
    ms = jnp.mean(x * x, axis=-1, keepdims=True)
    hb = (x * lax.rsqrt(ms + EPS) * ng_ref[...]).astype(BF16)

    def seg(c0, n):
        return _dot(hb, w_ref[:, c0:c0 + n])

    def store_kv(ref, t):
        if kv_by_column:
            ref[0] = t.T
        else:
            ref[...] = t

    def head_norm(t, g_ref):
        hi, lo = _split_bf16(t * t)
        gm = gmat_ref[...]
        msq = _dot(hi, gm) + _dot(lo, gm)
        return t * lax.rsqrt(msq + EPS) * g_ref[...]

    c = 0
    qa = head_norm(seg(c, w_a), qn_ref); c += w_a
    qa = qa * SCALE
    qah, qal = _split_bf16(qa)
    qah_ref[...] = qah
    qal_ref[...] = qal
    ka = head_norm(seg(c, w_a), kn_ref); c += w_a
    store_kv(ka_ref, ka)
    kab_ref[...] = ka.astype(BF16)
    km_ref[0] = jnp.mean(ka, axis=0, keepdims=True)
    va = seg(c, w_a); c += w_a
    store_kv(va_ref, va)
    vab_ref[...] = va.astype(BF16)
    qb = seg(c, w_b); c += w_b
    qbb_ref[...] = (qb * (SCALE * LOG2_E)).astype(BF16)
    kb = seg(c, w_b); c += w_b
    store_kv(kb_ref, kb)
    kbb_ref[...] = kb.astype(BF16)
    vb = seg(c, w_b); c += w_b
    store_kv(vb_ref, vb)
    vbb_ref[...] = vb.astype(BF16)
    ga_ref[...] = seg(c, d_model); c += d_model
    gb_ref[...] = seg(c, d_model)


def _proj(x, norm_g, w_in_bf, gmat, qn_t, kn_t, w_a, w_b, seq_len=None):
    n, d_model = x.shape
    ts = TOKEN_TILE
    steps = n // ts
    row = lambda w: pl.BlockSpec((ts, w), lambda i: (i, 0))
    f32 = lambda w: jax.ShapeDtypeStruct((n, w), F32)
    bf = lambda w: jax.ShapeDtypeStruct((n, w), BF16)
    if seq_len is None:
        kv, kv_spec = f32, row
    else:
        per_seq = seq_len // ts
        kv = lambda w: jax.ShapeDtypeStruct((n // seq_len, w, seq_len), F32)
        kv_spec = lambda w: pl.BlockSpec((1, w, ts), lambda i: (i // per_seq, 0, i % per_seq))
    out_shape = ([kv(w_a), kv(w_a), kv(w_b), kv(w_b), f32(d_model), f32(d_model)]
                 + [bf(w_a)] * 4 + [bf(w_b)] * 3
                 + [jax.ShapeDtypeStruct((steps, 1, w_a), F32)])
    out_specs = ([kv_spec(w_a), kv_spec(w_a), kv_spec(w_b), kv_spec(w_b), row(d_model), row(d_model)]
                 + [row(w_a)] * 4 + [row(w_b)] * 3
                 + [pl.BlockSpec((1, 1, w_a), lambda i: (i, 0, 0))])
    return pl.pallas_call(
        functools.partial(_proj_kernel, w_a=w_a, w_b=w_b, d_model=d_model,
                          kv_by_column=seq_len is not None),
        grid=(steps,),
        in_specs=[row(d_model), _const_spec((1, d_model)), _const_spec(w_in_bf.shape),
                  _const_spec(gmat.shape), _const_spec((1, w_a)), _const_spec((1, w_a))],
        out_specs=out_specs,
        out_shape=out_shape,
        compiler_params=_compiler_params(("parallel",)),
        name="proj",
    )(x, norm_g, w_in_bf, gmat, qn_t, kn_t)


def _prompt_bias_kernel(relb_ref, bkt_ref, out_ref):
    h = pl.program_id(0)
    b = bkt_ref[...]
    acc = jnp.zeros(b.shape, F32)
    for n in range(N_BUCKETS):
        acc = jnp.where(b == n, relb_ref[h, n], acc)
    out_ref[0] = acc


def _prompt_bias(relb_t):
    n_heads = relb_t.shape[0]
    r = np.arange(MOBA_BLOCK)[:, None]
    c = np.arange(MOBA_BLOCK)[None, :]
    bkt = np.stack([_t5_bucket_np(r - c), _t5_bucket_np(MOBA_BLOCK + r - c)])
    return pl.pallas_call(
        _prompt_bias_kernel,
        grid=(n_heads,),
        in_specs=[pl.BlockSpec(memory_space=pltpu.SMEM), _const_spec(bkt.shape)],
        out_specs=pl.BlockSpec((1,) + bkt.shape, lambda h: (h, 0, 0, 0)),
        out_shape=jax.ShapeDtypeStruct((n_heads,) + bkt.shape, F32),
        compiler_params=_compiler_params(("parallel",)),
        name="prompt_bias",
    )(relb_t, jnp.asarray(bkt))


def _sample_bias_kernel(relb_ref, idx_ref, out_ref, *, n_entries):
    idx = idx_ref[...]
    acc = jnp.zeros(idx.shape, F32)
    for n in range(n_entries):
        acc = jnp.where(idx == n, relb_ref[n], acc)
    out_ref[...] = acc


def _sample_bias(relb_flat, n_heads, t_new, past_len):
    rows = n_heads * t_new
    head = (np.arange(rows) // t_new)[:, None]
    t = (np.arange(rows) % t_new)[:, None]
    key = np.arange(MOBA_BLOCK)[None, :]
    far = np.full((rows, MOBA_BLOCK), N_BUCKETS - 1, np.int32)
    last = _t5_bucket_np(past_len + t - (past_len - MOBA_BLOCK + key))
    new = _t5_bucket_np(t - key)
    idx = (np.stack([far, last, new]) + head[None] * N_BUCKETS).astype(np.int32)
    return pl.pallas_call(
        functools.partial(_sample_bias_kernel, n_entries=n_heads * N_BUCKETS),
        in_specs=[pl.BlockSpec(memory_space=pltpu.SMEM), pl.BlockSpec(memory_space=pltpu.VMEM)],
        out_specs=pl.BlockSpec(memory_space=pltpu.VMEM),
        out_shape=jax.ShapeDtypeStruct(idx.shape, F32),
        name="sample_bias",
    )(relb_flat, jnp.asarray(idx))


def _top_blocks(gate, valid, lane, n_cols):
    gate = jnp.where(valid, gate, -jnp.inf)
    rank = jnp.zeros(gate.shape, jnp.int32)
    for j in range(n_cols):
        cj = gate[:, j:j + 1]
        beats = (cj > gate) | ((cj == gate) & (lane > j))
        rank = rank + beats.astype(jnp.int32)
    return valid & (rank < MOBA_TOPK)


def _moba_kernel(qh_ref, ql_ref, k_ref, v_ref, km_ref, bias_ref, far_ref, o_ref):
    i = pl.program_id(2)
    tq = qh_ref.shape[1]
    halves = tq // MOBA_BLOCK
    n_blocks = km_ref.shape[1]
    void_lane = n_blocks
    assert n_blocks % 8 == 0 and n_blocks + 8 <= HEAD_DIM
    first = i * halves
    lane = lax.broadcasted_iota(jnp.int32, (1, LANES), 1)
    row = lax.broadcasted_iota(jnp.int32, (tq, MOBA_BLOCK), 0)
    col = lax.broadcasted_iota(jnp.int32, (tq, MOBA_BLOCK), 1)
    blk = lax.broadcasted_iota(jnp.int32, (n_blocks, tq), 0)
    own = first + lax.broadcasted_iota(jnp.int32, (n_blocks, tq), 1) // MOBA_BLOCK
    km_hi, km_lo = _split_bf16(km_ref[0])
    mine = [(lane // HEAD_DIM) == hs for hs in range(HEADS_PER_TILE)]
    other = [(1 - hs) * HEAD_DIM for hs in range(HEADS_PER_TILE)]

    def query_with_terms(hs):
        qh = jnp.where(mine[hs], qh_ref[0], 0)
        ql = jnp.where(mine[hs], ql_ref[0], 0)
        gate = _dot_nt(km_hi, qh) + _dot_nt(km_hi, ql) + _dot_nt(km_lo, qh)
        valid = blk < own
        gate = jnp.where(valid, gate, -jnp.inf)
        rank = jnp.zeros(gate.shape, jnp.int32)
        for j in range(n_blocks):
            gj = gate[j:j + 1, :]
            beats = (gj > gate) | ((gj == gate) & (blk > j))
            rank = rank + beats.astype(jnp.int32)
        keep = (valid & (rank < MOBA_TOPK)) | (blk == own)
        term = jnp.where(keep, 0.0, NEG_SCORE)
        pad = lambda n, value: [jnp.full((n, tq), value, F32)] if n else []
        term = jnp.concatenate(pad(other[hs], 0.0) + [term] + pad(8, NEG_SCORE)
                               + pad(LANES - other[hs] - n_blocks - 8, 0.0), axis=0)
        return jnp.where(mine[hs], qh_ref[0], term.T.astype(BF16))

    def attend(state, q, hs, j, bias, mask, void):
        m, acc = state
        rows = pl.ds(pl.multiple_of(jnp.maximum(j, 0) * MOBA_BLOCK, MOBA_BLOCK), MOBA_BLOCK)
        hot = (lane == other[hs] + j) | (lane == jnp.where(void, other[hs] + void_lane, -1))
        kj = jnp.where(mine[hs], k_ref[0, rows, :], jnp.where(hot, 1.0, 0.0).astype(BF16))
        vj = jnp.where(mine[hs], v_ref[0, rows, :], jnp.ones((1, LANES), BF16))
        s = _dot_nt(q, kj) + bias
        if mask is not None:
            s = jnp.where(mask, s, -jnp.inf)
        m_new = jnp.maximum(m, jnp.max(s, axis=1, keepdims=True))
        pv = _dot(jnp.exp(s - m_new).astype(BF16), vj)
        return m_new, jnp.exp(m - m_new) * acc + pv

    def band_bias(hs, h):
        def part(r):
            if r == h:
                return bias_ref[hs, 0]
            if r == h + 1:
                return bias_ref[hs, 1]
            return jnp.broadcast_to(far_ref[hs], (MOBA_BLOCK, MOBA_BLOCK))
        return jnp.concatenate([part(r) for r in range(halves)], axis=0)

    qs = [query_with_terms(hs) for hs in range(HEADS_PER_TILE)]
    states = [(jnp.full((tq, 1), MAX_INIT, F32), jnp.zeros((tq, LANES), F32))
              for _ in range(HEADS_PER_TILE)]
    for h in range(halves):
        causal = (row // MOBA_BLOCK != h) | (col <= row - h * MOBA_BLOCK)
        states = [attend(states[hs], qs[hs], hs, first + h, band_bias(hs, h), causal, False)
                  for hs in range(HEADS_PER_TILE)]
    states = [attend(states[hs], qs[hs], hs, first - 1, band_bias(hs, -1), None, i == 0)
              for hs in range(HEADS_PER_TILE)]

    def far_body(j, flat):
        out = []
        for hs in range(HEADS_PER_TILE):
            out += attend((flat[2 * hs], flat[2 * hs + 1]), qs[hs], hs, j, far_ref[hs], None, False)
        return tuple(out)

    flat = lax.fori_loop(0, jnp.maximum(first - 1, 0), far_body,
                         tuple(x for st in states for x in st))
    outs = [flat[2 * hs + 1] / flat[2 * hs + 1][:, other[hs]:other[hs] + 1]
            for hs in range(HEADS_PER_TILE)]
    o_ref[0] = jnp.where(lane < HEAD_DIM, outs[0], outs[1]).astype(o_ref.dtype)


def _moba_prompt(qh, ql, kb, vb, km, bias, far):
    b, s, w = qh.shape
    n_tiles = w // LANES
    nq = s // QUERY_TILE
    q_spec = pl.BlockSpec((1, QUERY_TILE, LANES), lambda bi, hp, i: (bi, i, hp))
    kv_spec = pl.BlockSpec((1, s, LANES), lambda bi, hp, i: (bi, 0, hp))
    return pl.pallas_call(
        _moba_kernel,
        grid=(b, n_tiles, nq),
        in_specs=[q_spec, q_spec, kv_spec, kv_spec,
                  pl.BlockSpec((1, km.shape[1], LANES), lambda bi, hp, i: (bi, 0, hp)),
                  pl.BlockSpec((HEADS_PER_TILE, 2, MOBA_BLOCK, MOBA_BLOCK),
                               lambda bi, hp, i: (hp, 0, 0, 0)),
                  pl.BlockSpec((HEADS_PER_TILE, 1, MOBA_BLOCK), lambda bi, hp, i: (hp, 0, 0))],
        out_specs=q_spec,
        out_shape=jax.ShapeDtypeStruct((b, s, w), BF16),
        compiler_params=_compiler_params(("parallel", "parallel", "arbitrary")),
        name="moba_prompt",
    )(qh, ql, kb, vb, km, bias, far)


def _softplus2(z2):
    return jnp.maximum(z2, 0.0) + jnp.log2(1.0 + jnp.exp2(-jnp.abs(z2)))


def _suffix_sums(x, tri2):
    hi, lo = _split_bf16(x)
    return _dot(jnp.concatenate([hi, lo], axis=1), tri2)


def _sb_weights(z2, incl, carry, n_sub):
    rows, sub = z2.shape[0], incl.shape[1]
    out = [None] * n_sub
    for c in reversed(range(n_sub)):
        inc = incl[c * rows:(c + 1) * rows]
        out[c] = jnp.exp2(z2[:, c * sub:(c + 1) * sub] - inc + carry)
        carry = carry - inc[:, 0:1]
    return jnp.concatenate(out, axis=1), carry


def _sb_kernel(q_ref, k_ref, v_ref, tri_ref, o_ref):
    i = pl.program_id(2)
    tq = q_ref.shape[1]
    sub = tri_ref.shape[1]
    n_sub = tq // sub
    lane = lax.broadcasted_iota(jnp.int32, (1, LANES), 1)
    row = lax.broadcasted_iota(jnp.int32, (tq, tq), 0)
    col = lax.broadcasted_iota(jnp.int32, (tq, tq), 1)

    def tile(state, qh, j, past):
        carry, acc = state
        rows = pl.ds(pl.multiple_of(j * tq, tq), tq)
        z2 = _dot_nt(qh, k_ref[0, rows, :])
        sp = _softplus2(z2)
        if past is not None:
            sp = jnp.where(past, sp, 0.0)
        stacked = jnp.concatenate([sp[:, c * sub:(c + 1) * sub] for c in range(n_sub)], axis=0)
        a, carry = _sb_weights(z2, _suffix_sums(stacked, tri_ref[...]), carry, n_sub)
        if past is not None:
            a = jnp.where(past, a, 0.0)
        return carry, acc + _dot(a.astype(BF16), v_ref[0, rows, :])

    heads = range(HEADS_PER_TILE)
    qs = [jnp.where((lane // HEAD_DIM) == hs, q_ref[0], 0) for hs in heads]
    states = [tile((jnp.zeros((tq, 1), F32), jnp.zeros((tq, LANES), F32)), qs[hs], i, col < row)
              for hs in heads]

    def live(flat):
        return jnp.maximum(jnp.max(flat[0]), jnp.max(flat[2])) > F32_EXP2_UNDERFLOW

    def body(c):
        t, flat = c[0], c[1:]
        out = []
        for hs in heads:
            out += tile((flat[2 * hs], flat[2 * hs + 1]), qs[hs], i - 1 - t, None)
        return (t + 1, *out)

    flat = tuple(x for st in states for x in st)
    flat = lax.while_loop(lambda c: (c[0] < i) & live(c[1:]), body, (jnp.int32(0), *flat))[1:]
    o_ref[0] = jnp.where(lane < HEAD_DIM, flat[1], flat[3]).astype(o_ref.dtype)


def _sb_prompt(q, kb, vb, tri):
    b, s, w = q.shape
    n_tiles = w // LANES
    tq = QUERY_TILE
    assert tq % tri.shape[1] == 0 and HEADS_PER_TILE == 2
    q_spec = pl.BlockSpec((1, tq, LANES), lambda bi, hp, i: (bi, i, hp))
    kv_spec = pl.BlockSpec((1, s, LANES), lambda bi, hp, i: (bi, 0, hp))
    return pl.pallas_call(
        _sb_kernel,
        grid=(b, n_tiles, s // tq),
        in_specs=[q_spec, kv_spec, kv_spec, _const_spec(tri.shape)],
        out_specs=q_spec,
        out_shape=jax.ShapeDtypeStruct((b, s, w), BF16),
        compiler_params=_compiler_params(("parallel", "parallel", "arbitrary")),
        name="sb_prompt",
    )(q, kb, vb, tri)


def _sample_kernel(pt_ref, qa_ref, qb_ref, kan_ref, van_ref, kbn_ref, vbn_ref, bias_ref, tri_ref,
                   *rest, t_new, n_heads, n_past_blocks):
    pps = PAGES_PER_STEP
    ka_pages, va_pages = rest[0:pps], rest[pps:2 * pps]
    kb_pages, vb_pages = rest[2 * pps:3 * pps], rest[3 * pps:4 * pps]
    oa_ref, ob_ref = rest[4 * pps], rest[4 * pps + 1]
    (qab_sc, qbb_sc, oblk_sc, gate_sc, max_sc, sum_sc, carry_sc, accb_sc) = rest[4 * pps + 2:]

    g = pl.program_id(1)
    rows = n_heads * t_new
    width = qa_ref.shape[2]
    n_blk = pps // PAGES_PER_BLOCK
    own = n_past_blocks
    lane = lax.broadcasted_iota(jnp.int32, (rows, LANES), 1)

    def record(entries):
        for ref, pos in ((gate_sc, 1), (max_sc, 2), (sum_sc, 3)):
            new = ref[...]
            for e in entries:
                new = jnp.where(lane == e[0], e[pos], new)
            ref[...] = new

    def softmax_partial(s, bias, mask):
        s = s[0:rows] + s[rows:2 * rows]
        gate = jnp.sum(s, axis=1, keepdims=True)
        s = s + bias
        if mask is not None:
            s = jnp.where(mask, s, -jnp.inf)
        m = jnp.max(s, axis=1, keepdims=True)
        p = jnp.exp(s - m)
        return p.astype(BF16), gate, m, jnp.sum(p, axis=1, keepdims=True)

    @pl.when(g == 0)
    def _init():
        r_h = lax.broadcasted_iota(jnp.int32, (rows, width), 0) // t_new
        c_h = lax.broadcasted_iota(jnp.int32, (rows, width), 1) // HEAD_DIM
        own_head = r_h == c_h

        def expand(q):
            return jnp.where(own_head, jnp.concatenate([q] * n_heads, axis=0), 0.0)

        qa_hi, qa_lo = _split_bf16(expand(qa_ref[0]))
        qab_sc[0:rows] = qa_hi
        qab_sc[rows:2 * rows] = qa_lo
        qb = expand(qb_ref[0]).astype(BF16)
        qbb_sc[...] = qb
        for ref in (gate_sc, max_sc, sum_sc):
            ref[...] = jnp.zeros_like(ref)

        def page_of(x):
            return jnp.concatenate([x, jnp.zeros((PAGE_SIZE - t_new, width), F32)], axis=0).astype(BF16)

        r_t = lax.broadcasted_iota(jnp.int32, (rows, PAGE_SIZE), 0) % t_new
        key = lax.broadcasted_iota(jnp.int32, (rows, PAGE_SIZE), 1)
        p, gate, m, l = softmax_partial(_dot_nt(qab_sc[...], page_of(kan_ref[0])),
                                        bias_ref[2, :, 0:PAGE_SIZE], key <= r_t)
        oblk_sc[own] = _dot(p, page_of(van_ref[0]))
        record([(own, gate, m, l)])

        past = key < r_t
        z = _dot_nt(qb, page_of(kbn_ref[0]))
        hi, lo = _split_bf16(jnp.where(past, _softplus2(z), 0.0))
        tri_page = tri_ref[0:PAGE_SIZE, 0:PAGE_SIZE]
        incl = _dot(hi, tri_page) + _dot(lo, tri_page)
        a = jnp.where(past, jnp.exp2(z - incl), 0.0)
        accb_sc[...] = _dot(a.astype(BF16), page_of(vbn_ref[0]))
        carry_sc[...] = -incl[:, 0:1]

    first_block = n_past_blocks - n_blk * (g + 1)
    cat = lambda refs: jnp.concatenate([r[0] for r in refs], axis=1).astype(BF16)
    keys_of = lambda c: slice(c * MOBA_BLOCK, (c + 1) * MOBA_BLOCK)

    @pl.when(jnp.max(carry_sc[...]) > F32_EXP2_UNDERFLOW)
    def _stick_breaking():
        z = _dot(qbb_sc[...], cat(kb_pages))
        sp = _softplus2(z)
        stacked = jnp.concatenate([sp[:, keys_of(c)] for c in range(n_blk)], axis=0)
        a, carry = _sb_weights(z, _suffix_sums(stacked, tri_ref[...]), carry_sc[...], n_blk)
        carry_sc[...] = carry
        accb_sc[...] += _dot_nt(a.astype(BF16), cat(vb_pages))

    s_all = _dot(qab_sc[...], cat(ka_pages))
    va = cat(va_pages)
    entries = []
    for c in range(n_blk):
        j = first_block + c
        bias = bias_ref[jnp.where(j == n_past_blocks - 1, 1, 0)]
        p, gate, m, l = softmax_partial(s_all[:, keys_of(c)], bias, None)
        oblk_sc[j] = _dot_nt(p, va[:, keys_of(c)])
        entries.append((j, gate, m, l))
    record(entries)

    @pl.when(g == pl.num_programs(1) - 1)
    def _finish():
        sel = _top_blocks(gate_sc[...], lane < n_past_blocks, lane, n_past_blocks) | (lane == own)
        m_blk = max_sc[...]
        m_fin = jnp.max(jnp.where(sel, m_blk, -jnp.inf), axis=1, keepdims=True)
        wgt = jnp.where(sel, jnp.exp(m_blk - m_fin), 0.0)
        denom = jnp.sum(wgt * sum_sc[...], axis=1, keepdims=True)
        o = wgt[:, 0:1] * oblk_sc[0]
        for j in range(1, own + 1):
            o = o + wgt[:, j:j + 1] * oblk_sc[j]
        o = o / denom
        r_h = lax.broadcasted_iota(jnp.int32, (rows, width), 0) // t_new
        c_h = lax.broadcasted_iota(jnp.int32, (rows, width), 1) // HEAD_DIM
        own_head = r_h == c_h

        def fold(x):
            x = jnp.where(own_head, x, 0.0)
            out = x[0:t_new]
            for h in range(1, n_heads):
                out = out + x[h * t_new:(h + 1) * t_new]
            return out

        oa_ref[0] = fold(o)
        ob_ref[0] = fold(accb_sc[...])


def _sample_mixers(page_table, qa, qb, ka_new, va_new, kb_new, vb_new,
                   ck_a, cv_a, ck_b, cv_b, bias_tabs, tri):
    bd, t_new, width = qa.shape
    n_heads = width // HEAD_DIM
    n_pages = page_table.shape[1]
    n_past_blocks = n_pages // PAGES_PER_BLOCK
    steps = n_pages // PAGES_PER_STEP
    rows = n_heads * t_new
    assert n_past_blocks + 1 <= LANES and n_pages % PAGES_PER_STEP == 0

    tok_spec = pl.BlockSpec((1, t_new, width), lambda b, g, pt: (b, 0, 0))

    def page_spec(k):
        def index(b, g, pt):
            return (pt[b, n_pages - PAGES_PER_STEP * (g + 1) + k], 0, 0)
        return pl.BlockSpec((1, width, PAGE_SIZE), index)

    page_specs = [page_spec(k) for k in range(PAGES_PER_STEP)]
    const = lambda shape: pl.BlockSpec(shape, lambda b, g, pt: (0,) * len(shape),
                                       pipeline_mode=pl.Buffered(1))
    grid_spec = pltpu.PrefetchScalarGridSpec(
        num_scalar_prefetch=1,
        grid=(bd, steps),
        in_specs=[tok_spec] * 6 + [const(bias_tabs.shape), const(tri.shape)] + page_specs * 4,
        out_specs=[tok_spec, tok_spec],
        scratch_shapes=[pltpu.VMEM((2 * rows, width), BF16), pltpu.VMEM((rows, width), BF16),
                        pltpu.VMEM((n_past_blocks + 1, rows, width), F32),
                        pltpu.VMEM((rows, LANES), F32), pltpu.VMEM((rows, LANES), F32),
                        pltpu.VMEM((rows, LANES), F32), pltpu.VMEM((rows, 1), F32),
                        pltpu.VMEM((rows, width), F32)],
    )
    out = jax.ShapeDtypeStruct((bd, t_new, width), F32)
    pages = ([ck_a] * PAGES_PER_STEP + [cv_a] * PAGES_PER_STEP
             + [ck_b] * PAGES_PER_STEP + [cv_b] * PAGES_PER_STEP)
    return pl.pallas_call(
        functools.partial(_sample_kernel, t_new=t_new, n_heads=n_heads, n_past_blocks=n_past_blocks),
        grid_spec=grid_spec,
        out_shape=[out, out],
        compiler_params=_compiler_params(("parallel", "arbitrary")),
        name="sample_mixers",
    )(page_table, qa, qb, ka_new, va_new, kb_new, vb_new, bias_tabs, tri, *pages)


def _mlp_kernel(x_ref, oa_ref, ob_ref, ga_ref, gb_ref, wa_ref, wb_ref, wo_ref, ng_ref, wu_ref, wd_ref,
                y_ref):
    branch_a = _dot(oa_ref[...].astype(BF16), wa_ref[...])
    branch_b = _dot(ob_ref[...].astype(BF16), wb_ref[...])
    mixed = jax.nn.sigmoid(ga_ref[...]) * branch_a + jax.nn.sigmoid(gb_ref[...]) * branch_b
    x = x_ref[...] + _dot(mixed.astype(BF16), wo_ref[...])
    ms = jnp.mean(x * x, axis=-1, keepdims=True)
    h = (x * lax.rsqrt(ms + EPS) * ng_ref[...]).astype(BF16)
    up = jnp.maximum(_dot(h, wu_ref[...]), 0.0)
    y_ref[...] = x + _dot((up * up).astype(BF16), wd_ref[...])


def _merge_mlp(x, oa, ob, ga, gb, wa, wb, wo, ng, wu, wd):
    n, d_model = x.shape
    ts = TOKEN_TILE
    row = lambda w: pl.BlockSpec((ts, w), lambda i: (i, 0))
    return pl.pallas_call(
        _mlp_kernel,
        grid=(n // ts,),
        in_specs=[row(d_model), row(oa.shape[1]), row(ob.shape[1]), row(d_model), row(d_model),
                  _const_spec(wa.shape), _const_spec(wb.shape), _const_spec(wo.shape),
                  _const_spec(ng.shape), _const_spec(wu.shape), _const_spec(wd.shape)],
        out_specs=row(d_model),
        out_shape=jax.ShapeDtypeStruct((n, d_model), F32),
        compiler_params=_compiler_params(("parallel",)),
        name="merge_mlp",
    )(x, oa, ob, ga, gb, wa, wb, wo, ng, wu, wd)


def kernel(x_prompt, x_sample, cache_k_a, cache_v_a, cache_k_b, cache_v_b, page_table, rel_bias,
           norm_attn, w_in, q_norm, k_norm, w_br_a, w_br_b, w_o, norm_mlp, w_up, w_down):
    depth = w_in.shape[0]
    batch, seq, d_model = x_prompt.shape
    dec_batch, t_new, _ = x_sample.shape
    n_heads_a = cache_k_a.shape[3]
    n_heads_b = cache_k_b.shape[3]
    w_a, w_b = n_heads_a * HEAD_DIM, n_heads_b * HEAD_DIM
    assert w_a == w_b and n_heads_a == n_heads_b
    n_phys = cache_k_a.shape[1]
    past_len = page_table.shape[1] * PAGE_SIZE
    assert seq % MOBA_BLOCK == 0 and past_len % MOBA_BLOCK == 0 and t_new <= PAGE_SIZE
    assert TOKEN_TILE == MOBA_BLOCK and seq % QUERY_TILE == 0

    relb_t = rel_bias.T.astype(F32)
    bias_prompt = _prompt_bias(relb_t)
    far_prompt = jnp.broadcast_to(relb_t[:, N_BUCKETS - 1][:, None, None], (n_heads_a, 1, MOBA_BLOCK))
    bias_sample = _sample_bias(relb_t.reshape(-1), n_heads_a, t_new, past_len)
    head_of = np.arange(w_a) // HEAD_DIM
    gmat = jnp.asarray((head_of[:, None] == head_of[None, :]) / HEAD_DIM, BF16)
    idx = np.arange(MOBA_BLOCK)
    tri = idx[:, None] >= idx[None, :]
    tri = jnp.asarray(np.concatenate([tri, tri], axis=0), BF16)

    xp = x_prompt.reshape(batch * seq, d_model)
    xs = x_sample.reshape(dec_batch * t_new, d_model)
    prompt_kv, sample_kv = [], []
    for layer in range(depth):
        ng = norm_attn[layer][None]
        w_in_bf = w_in[layer].astype(BF16)
        qn_t = jnp.tile(q_norm[layer], n_heads_a)[None]
        kn_t = jnp.tile(k_norm[layer], n_heads_a)[None]
        mlp_w = (w_br_a[layer].astype(BF16), w_br_b[layer].astype(BF16), w_o[layer].astype(BF16),
                 norm_mlp[layer][None], w_up[layer].astype(BF16), w_down[layer].astype(BF16))

        (ka, va, kb, vb, ga, gb, qah, qal, kab, vab, qbb, kbb, vbb, km) = _proj(
            xp, ng, w_in_bf, gmat, qn_t, kn_t, w_a, w_b, seq_len=seq)
        seq3 = lambda t: t.reshape(batch, seq, t.shape[-1])
        oa = _moba_prompt(seq3(qah), seq3(qal), seq3(kab), seq3(vab),
                          km.reshape(batch, seq // MOBA_BLOCK, w_a), bias_prompt, far_prompt)
        ob = _sb_prompt(seq3(qbb), seq3(kbb), seq3(vbb), tri)
        xp = _merge_mlp(xp, oa.reshape(-1, w_a), ob.reshape(-1, w_b), ga, gb, *mlp_w)
        heads = lambda t, n: jnp.transpose(t.reshape(batch, n, HEAD_DIM, seq), (0, 3, 1, 2))
        prompt_kv.append((heads(ka, n_heads_a), heads(va, n_heads_a),
                          heads(kb, n_heads_b), heads(vb, n_heads_b)))

        (ka, va, kb, vb, ga, gb, qah, qal, _, _, qbb, _, _, _) = _proj(
            xs, ng, w_in_bf, gmat, qn_t, kn_t, w_a, w_b)
        tok3 = lambda t: t.reshape(dec_batch, t_new, t.shape[-1])
        qa_f32 = qah.astype(F32) + qal.astype(F32)
        pool = lambda c: jnp.transpose(c, (0, 1, 3, 4, 2)).reshape(depth * n_phys, -1, PAGE_SIZE)
        oa, ob = _sample_mixers(page_table + layer * n_phys, tok3(qa_f32), tok3(qbb.astype(F32)),
                                tok3(ka), tok3(va), tok3(kb), tok3(vb),
                                pool(cache_k_a), pool(cache_v_a), pool(cache_k_b), pool(cache_v_b),
                                bias_sample, tri)
        xs = _merge_mlp(xs, oa.reshape(-1, w_a), ob.reshape(-1, w_b), ga, gb, *mlp_w)
        heads = lambda t, n: t.reshape(dec_batch, t_new, n, HEAD_DIM)
        sample_kv.append((heads(ka, n_heads_a), heads(va, n_heads_a),
                          heads(kb, n_heads_b), heads(vb, n_heads_b)))

    stack = lambda kvs, i: jnp.stack([kv[i] for kv in kvs])
    return (xp.reshape(batch, seq, d_model), xs.reshape(dec_batch, t_new, d_model),
            *(stack(prompt_kv, i) for i in range(4)), *(stack(sample_kv, i) for i in range(4)))
```

```python
import functools
import math

import jax
import jax.numpy as jnp
import numpy as np
from jax import lax
from jax.experimental import pallas as pl
from jax.experimental.pallas import tpu as pltpu

F32 = jnp.float32
BF16 = jnp.bfloat16

HEAD_DIM = 64
MOBA_BLOCK = 256
MOBA_TOPK = 3
N_BUCKETS = 32
MAX_DISTANCE = 128
EPS = 1e-6
SCALE = HEAD_DIM ** -0.5
LOG2_E = math.log2(math.e)
PAGE_SIZE = 128

LANES = 128
HEADS_PER_TILE = LANES // HEAD_DIM
TOKEN_TILE = 256
QUERY_TILE = 512
NEG_SCORE = -1e30
MAX_INIT = -1e29
F32_EXP2_UNDERFLOW = -160.0
PAGES_PER_STEP = 16
PAGES_PER_BLOCK = MOBA_BLOCK // PAGE_SIZE
VMEM_LIMIT = 56 * 1024 * 1024

NT_DIMS = (((1,), (1,)), ((), ()))


def _dot(a, b):
    return jnp.dot(a, b, preferred_element_type=F32)


def _dot_nt(a, b):
    return lax.dot_general(a, b, NT_DIMS, preferred_element_type=F32)


def _split_bf16(x):
    hi = x.astype(BF16)
    lo = (x - hi.astype(F32)).astype(BF16)
    return hi, lo


def _t5_bucket_np(dist):
    max_exact = N_BUCKETS // 2
    d = np.maximum(dist, 0)
    df = np.maximum(d, 1).astype(np.float32)
    large = max_exact + (np.log(df / max_exact) / math.log(MAX_DISTANCE / max_exact)
                         * (N_BUCKETS - max_exact)).astype(np.int32)
    large = np.minimum(large, N_BUCKETS - 1)
    return np.where(d < max_exact, d, large).astype(np.int32)


def _compiler_params(semantics):
    return pltpu.CompilerParams(dimension_semantics=semantics, vmem_limit_bytes=VMEM_LIMIT)


def _const_spec(shape):
    zeros = (0,) * len(shape)
    return pl.BlockSpec(shape, lambda *_: zeros, pipeline_mode=pl.Buffered(1))


def _proj_kernel(x_ref, ng_ref, w_ref, gmat_ref, qn_ref, kn_ref,
                 ka_ref, va_ref, kb_ref, vb_ref, ga_ref, gb_ref,
                 qah_ref, qal_ref, kab_ref, vab_ref, qbb_ref, kbb_ref, vbb_ref, km_ref,
                 *, w_a, w_b, d_model, kv_by_column):
    x = x_ref[...]
    ms = jnp.mean(x * x, axis=-1, keepdims=True)
    hb = (x * lax.rsqrt(ms + EPS) * ng_ref[...]).astype(BF16)

    def seg(c0, n):
        return _dot(hb, w_ref[:, c0:c0 + n])

    def store_kv(ref, t):
        if kv_by_column:
            ref[0] = t.T
        else:
            ref[...] = t

    def head_norm(t, g_ref):
        hi, lo = _split_bf16(t * t)
        gm = gmat_ref[...]
        msq = _dot(hi, gm) + _dot(lo, gm)
        return t * lax.rsqrt(msq + EPS) * g_ref[...]

    c = 0
    qa = head_norm(seg(c, w_a), qn_ref); c += w_a
    qa = qa * SCALE
    qah, qal = _split_bf16(qa)
    qah_ref[...] = qah
    qal_ref[...] = qal
    ka = head_norm(seg(c, w_a), kn_ref); c += w_a
    store_kv(ka_ref, ka)
    kab_ref[...] = ka.astype(BF16)
    km_ref[0] = jnp.mean(ka, axis=0, keepdims=True)
    va = seg(c, w_a); c += w_a
    store_kv(va_ref, va)
    vab_ref[...] = va.astype(BF16)
    qb = seg(c, w_b); c += w_b
    qbb_ref[...] = (qb * (SCALE * LOG2_E)).astype(BF16)
    kb = seg(c, w_b); c += w_b
    store_kv(kb_ref, kb)
    kbb_ref[...] = kb.astype(BF16)
    vb = seg(c, w_b); c += w_b
    store_kv(vb_ref, vb)
    vbb_ref[...] = vb.astype(BF16)
    ga_ref[...] = seg(c, d_model); c += d_model
    gb_ref[...] = seg(c, d_model)


def _proj(x, norm_g, w_in_bf, gmat, qn_t, kn_t, w_a, w_b, seq_len=None):
    n, d_model = x.shape
    ts = TOKEN_TILE
    steps = n // ts
    row = lambda w: pl.BlockSpec((ts, w), lambda i: (i, 0))
    f32 = lambda w: jax.ShapeDtypeStruct((n, w), F32)
    bf = lambda w: jax.ShapeDtypeStruct((n, w), BF16)
    if seq_len is None:
        kv, kv_spec = f32, row
    else:
        per_seq = seq_len // ts
        kv = lambda w: jax.ShapeDtypeStruct((n // seq_len, w, seq_len), F32)
        kv_spec = lambda w: pl.BlockSpec((1, w, ts), lambda i: (i // per_seq, 0, i % per_seq))
    out_shape = ([kv(w_a), kv(w_a), kv(w_b), kv(w_b), f32(d_model), f32(d_model)]
                 + [bf(w_a)] * 4 + [bf(w_b)] * 3
                 + [jax.ShapeDtypeStruct((steps, 1, w_a), F32)])
    out_specs = ([kv_spec(w_a), kv_spec(w_a), kv_spec(w_b), kv_spec(w_b), row(d_model), row(d_model)]
                 + [row(w_a)] * 4 + [row(w_b)] * 3
                 + [pl.BlockSpec((1, 1, w_a), lambda i: (i, 0, 0))])
    return pl.pallas_call(
        functools.partial(_proj_kernel, w_a=w_a, w_b=w_b, d_model=d_model,
                          kv_by_column=seq_len is not None),
        grid=(steps,),
        in_specs=[row(d_model), _const_spec((1, d_model)), _const_spec(w_in_bf.shape),
                  _const_spec(gmat.shape), _const_spec((1, w_a)), _const_spec((1, w_a))],
        out_specs=out_specs,
        out_shape=out_shape,
        compiler_params=_compiler_params(("parallel",)),
        name="proj",
    )(x, norm_g, w_in_bf, gmat, qn_t, kn_t)


def _prompt_bias_kernel(relb_ref, bkt_ref, out_ref):
    h = pl.program_id(0)
    b = bkt_ref[...]
    acc = jnp.zeros(b.shape, F32)
    for n in range(N_BUCKETS):
        acc = jnp.where(b == n, relb_ref[h, n], acc)
    out_ref[0] = acc


def _prompt_bias(relb_t):
    n_heads = relb_t.shape[0]
    r = np.arange(MOBA_BLOCK)[:, None]
    c = np.arange(MOBA_BLOCK)[None, :]
    bkt = np.stack([_t5_bucket_np(r - c), _t5_bucket_np(MOBA_BLOCK + r - c)])
    return pl.pallas_call(
        _prompt_bias_kernel,
        grid=(n_heads,),
        in_specs=[pl.BlockSpec(memory_space=pltpu.SMEM), _const_spec(bkt.shape)],
        out_specs=pl.BlockSpec((1,) + bkt.shape, lambda h: (h, 0, 0, 0)),
        out_shape=jax.ShapeDtypeStruct((n_heads,) + bkt.shape, F32),
        compiler_params=_compiler_params(("parallel",)),
        name="prompt_bias",
    )(relb_t, jnp.asarray(bkt))


def _sample_bias_kernel(relb_ref, idx_ref, out_ref, *, n_entries):
    idx = idx_ref[...]
    acc = jnp.zeros(idx.shape, F32)
    for n in range(n_entries):
        acc = jnp.where(idx == n, relb_ref[n], acc)
    out_ref[...] = acc


def _sample_bias(relb_flat, n_heads, t_new, past_len):
    rows = n_heads * t_new
    head = (np.arange(rows) // t_new)[:, None]
    t = (np.arange(rows) % t_new)[:, None]
    key = np.arange(MOBA_BLOCK)[None, :]
    far = np.full((rows, MOBA_BLOCK), N_BUCKETS - 1, np.int32)
    last = _t5_bucket_np(past_len + t - (past_len - MOBA_BLOCK + key))
    new = _t5_bucket_np(t - key)
    idx = (np.stack([far, last, new]) + head[None] * N_BUCKETS).astype(np.int32)
    return pl.pallas_call(
        functools.partial(_sample_bias_kernel, n_entries=n_heads * N_BUCKETS),
        in_specs=[pl.BlockSpec(memory_space=pltpu.SMEM), pl.BlockSpec(memory_space=pltpu.VMEM)],
        out_specs=pl.BlockSpec(memory_space=pltpu.VMEM),
        out_shape=jax.ShapeDtypeStruct(idx.shape, F32),
        name="sample_bias",
    )(relb_flat, jnp.asarray(idx))


def _top_blocks(gate, valid, lane, n_cols):
    gate = jnp.where(valid, gate, -jnp.inf)
    rank = jnp.zeros(gate.shape, jnp.int32)
    for j in range(n_cols):
        cj = gate[:, j:j + 1]
        beats = (cj > gate) | ((cj == gate) & (lane > j))
        rank = rank + beats.astype(jnp.int32)
    return valid & (rank < MOBA_TOPK)


def _moba_kernel(qh_ref, ql_ref, k_ref, v_ref, km_ref, bias_ref, far_ref, o_ref):
    i = pl.program_id(2)
    tq = qh_ref.shape[1]
    halves = tq // MOBA_BLOCK
    n_blocks = km_ref.shape[1]
    void_lane = n_blocks
    assert n_blocks % 8 == 0 and n_blocks + 8 <= HEAD_DIM
    first = i * halves
    lane = lax.broadcasted_iota(jnp.int32, (1, LANES), 1)
    row = lax.broadcasted_iota(jnp.int32, (tq, MOBA_BLOCK), 0)
    col = lax.broadcasted_iota(jnp.int32, (tq, MOBA_BLOCK), 1)
    blk = lax.broadcasted_iota(jnp.int32, (n_blocks, tq), 0)
    own = first + lax.broadcasted_iota(jnp.int32, (n_blocks, tq), 1) // MOBA_BLOCK
    km_hi, km_lo = _split_bf16(km_ref[0])
    mine = [(lane // HEAD_DIM) == hs for hs in range(HEADS_PER_TILE)]
    other = [(1 - hs) * HEAD_DIM for hs in range(HEADS_PER_TILE)]

    def query_with_terms(hs):
        qh = jnp.where(mine[hs], qh_ref[0], 0)
        ql = jnp.where(mine[hs], ql_ref[0], 0)
        gate = _dot_nt(km_hi, qh) + _dot_nt(km_hi, ql) + _dot_nt(km_lo, qh)
        valid = blk < own
        gate = jnp.where(valid, gate, -jnp.inf)
        rank = jnp.zeros(gate.shape, jnp.int32)
        for j in range(n_blocks):
            gj = gate[j:j + 1, :]
            beats = (gj > gate) | ((gj == gate) & (blk > j))
            rank = rank + beats.astype(jnp.int32)
        keep = (valid & (rank < MOBA_TOPK)) | (blk == own)
        term = jnp.where(keep, 0.0, NEG_SCORE)
        pad = lambda n, value: [jnp.full((n, tq), value, F32)] if n else []
        term = jnp.concatenate(pad(other[hs], 0.0) + [term] + pad(8, NEG_SCORE)
                               + pad(LANES - other[hs] - n_blocks - 8, 0.0), axis=0)
        return jnp.where(mine[hs], qh_ref[0], term.T.astype(BF16))

    def attend(state, q, hs, j, bias, mask, void):
        m, acc = state
        rows = pl.ds(pl.multiple_of(jnp.maximum(j, 0) * MOBA_BLOCK, MOBA_BLOCK), MOBA_BLOCK)
        hot = (lane == other[hs] + j) | (lane == jnp.where(void, other[hs] + void_lane, -1))
        kj = jnp.where(mine[hs], k_ref[0, rows, :], jnp.where(hot, 1.0, 0.0).astype(BF16))
        vj = jnp.where(mine[hs], v_ref[0, rows, :], jnp.ones((1, LANES), BF16))
        s = _dot_nt(q, kj) + bias
        if mask is not None:
            s = jnp.where(mask, s, -jnp.inf)
        m_new = jnp.maximum(m, jnp.max(s, axis=1, keepdims=True))
        pv = _dot(jnp.exp(s - m_new).astype(BF16), vj)
        return m_new, jnp.exp(m - m_new) * acc + pv

    def band_bias(hs, h):
        def part(r):
            if r == h:
                return bias_ref[hs, 0]
            if r == h + 1:
                return bias_ref[hs, 1]
            return jnp.broadcast_to(far_ref[hs], (MOBA_BLOCK, MOBA_BLOCK))
        return jnp.concatenate([part(r) for r in range(halves)], axis=0)

    qs = [query_with_terms(hs) for hs in range(HEADS_PER_TILE)]
    states = [(jnp.full((tq, 1), MAX_INIT, F32), jnp.zeros((tq, LANES), F32))
              for _ in range(HEADS_PER_TILE)]
    for h in range(halves):
        causal = (row // MOBA_BLOCK != h) | (col <= row - h * MOBA_BLOCK)
        states = [attend(states[hs], qs[hs], hs, first + h, band_bias(hs, h), causal, False)
                  for hs in range(HEADS_PER_TILE)]
    states = [attend(states[hs], qs[hs], hs, first - 1, band_bias(hs, -1), None, i == 0)
              for hs in range(HEADS_PER_TILE)]

    def far_body(j, flat):
        out = []
        for hs in range(HEADS_PER_TILE):
            out += attend((flat[2 * hs], flat[2 * hs + 1]), qs[hs], hs, j, far_ref[hs], None, False)
        return tuple(out)

    flat = lax.fori_loop(0, jnp.maximum(first - 1, 0), far_body,
                         tuple(x for st in states for x in st))
    outs = [flat[2 * hs + 1] / flat[2 * hs + 1][:, other[hs]:other[hs] + 1]
            for hs in range(HEADS_PER_TILE)]
    o_ref[0] = jnp.where(lane < HEAD_DIM, outs[0], outs[1]).astype(o_ref.dtype)


def _moba_prompt(qh, ql, kb, vb, km, bias, far):
    b, s, w = qh.shape
    n_tiles = w // LANES
    nq = s // QUERY_TILE
    q_spec = pl.BlockSpec((1, QUERY_TILE, LANES), lambda bi, hp, i: (bi, i, hp))
    kv_spec = pl.BlockSpec((1, s, LANES), lambda bi, hp, i: (bi, 0, hp))
    return pl.pallas_call(
        _moba_kernel,
        grid=(b, n_tiles, nq),
        in_specs=[q_spec, q_spec, kv_spec, kv_spec,
                  pl.BlockSpec((1, km.shape[1], LANES), lambda bi, hp, i: (bi, 0, hp)),
                  pl.BlockSpec((HEADS_PER_TILE, 2, MOBA_BLOCK, MOBA_BLOCK),
                               lambda bi, hp, i: (hp, 0, 0, 0)),
                  pl.BlockSpec((HEADS_PER_TILE, 1, MOBA_BLOCK), lambda bi, hp, i: (hp, 0, 0))],
        out_specs=q_spec,
        out_shape=jax.ShapeDtypeStruct((b, s, w), BF16),
        compiler_params=_compiler_params(("parallel", "parallel", "arbitrary")),
        name="moba_prompt",
    )(qh, ql, kb, vb, km, bias, far)


def _softplus2(z2):
    return jnp.maximum(z2, 0.0) + jnp.log2(1.0 + jnp.exp2(-jnp.abs(z2)))


def _suffix_sums(x, tri2):
    hi, lo = _split_bf16(x)
    return _dot(jnp.concatenate([hi, lo], axis=1), tri2)


def _sb_weights(z2, incl, carry, n_sub):
    rows, sub = z2.shape[0], incl.shape[1]
    out = [None] * n_sub
    for c in reversed(range(n_sub)):
        inc = incl[c * rows:(c + 1) * rows]
        out[c] = jnp.exp2(z2[:, c * sub:(c + 1) * sub] - inc + carry)
        carry = carry - inc[:, 0:1]
    return jnp.concatenate(out, axis=1), carry


def _sb_kernel(q_ref, k_ref, v_ref, tri_ref, o_ref):
    i = pl.program_id(2)
    tq = q_ref.shape[1]
    sub = tri_ref.shape[1]
    n_sub = tq // sub
    lane = lax.broadcasted_iota(jnp.int32, (1, LANES), 1)
    row = lax.broadcasted_iota(jnp.int32, (tq, tq), 0)
    col = lax.broadcasted_iota(jnp.int32, (tq, tq), 1)

    def tile(state, qh, j, past):
        carry, acc = state
        rows = pl.ds(pl.multiple_of(j * tq, tq), tq)
        z2 = _dot_nt(qh, k_ref[0, rows, :])
        sp = _softplus2(z2)
        if past is not None:
            sp = jnp.where(past, sp, 0.0)
        stacked = jnp.concatenate([sp[:, c * sub:(c + 1) * sub] for c in range(n_sub)], axis=0)
        a, carry = _sb_weights(z2, _suffix_sums(stacked, tri_ref[...]), carry, n_sub)
        if past is not None:
            a = jnp.where(past, a, 0.0)
        return carry, acc + _dot(a.astype(BF16), v_ref[0, rows, :])

    heads = range(HEADS_PER_TILE)
    qs = [jnp.where((lane // HEAD_DIM) == hs, q_ref[0], 0) for hs in heads]
    states = [tile((jnp.zeros((tq, 1), F32), jnp.zeros((tq, LANES), F32)), qs[hs], i, col < row)
              for hs in heads]

    def live(flat):
        return jnp.maximum(jnp.max(flat[0]), jnp.max(flat[2])) > F32_EXP2_UNDERFLOW

    def body(c):
        t, flat = c[0], c[1:]
        out = []
        for hs in heads:
            out += tile((flat[2 * hs], flat[2 * hs + 1]), qs[hs], i - 1 - t, None)
        return (t + 1, *out)

    flat = tuple(x for st in states for x in st)
    flat = lax.while_loop(lambda c: (c[0] < i) & live(c[1:]), body, (jnp.int32(0), *flat))[1:]
    o_ref[0] = jnp.where(lane < HEAD_DIM, flat[1], flat[3]).astype(o_ref.dtype)


def _sb_prompt(q, kb, vb, tri):
    b, s, w = q.shape
    n_tiles = w // LANES
    tq = QUERY_TILE
    assert tq % tri.shape[1] == 0 and HEADS_PER_TILE == 2
    q_spec = pl.BlockSpec((1, tq, LANES), lambda bi, hp, i: (bi, i, hp))
    kv_spec = pl.BlockSpec((1, s, LANES), lambda bi, hp, i: (bi, 0, hp))
    return pl.pallas_call(
        _sb_kernel,
        grid=(b, n_tiles, s // tq),
        in_specs=[q_spec, kv_spec, kv_spec, _const_spec(tri.shape)],
        out_specs=q_spec,
        out_shape=jax.ShapeDtypeStruct((b, s, w), BF16),
        compiler_params=_compiler_params(("parallel", "parallel", "arbitrary")),
        name="sb_prompt",
    )(q, kb, vb, tri)


def _expand_heads(q, n_heads):
    t, width = q.shape
    r_h = lax.broadcasted_iota(jnp.int32, (n_heads * t, width), 0) // t
    c_h = lax.broadcasted_iota(jnp.int32, (n_heads * t, width), 1) // HEAD_DIM
    return jnp.where(r_h == c_h, jnp.concatenate([q] * n_heads, axis=0), 0.0)


def _sb_reach_kernel(pt_ref, qb_ref, *rest, n_heads):
    kb_pages, deep_ref = rest[:-1], rest[-1]
    qb = _expand_heads(qb_ref[0], n_heads).astype(BF16)
    kb = jnp.concatenate([r[0] for r in kb_pages], axis=1).astype(BF16)
    total = jnp.sum(_softplus2(_dot(qb, kb)), axis=1, keepdims=True)
    least = jnp.broadcast_to(jnp.min(total, axis=0, keepdims=True), (1, LANES))
    deep_ref[0] = (least < -F32_EXP2_UNDERFLOW).astype(jnp.int32)


def _sb_reach(page_table, qb, ck_b):
    bd, t_new, width = qb.shape
    n_pages = page_table.shape[1]

    def page_spec(k):
        return pl.BlockSpec((1, width, PAGE_SIZE),
                            lambda b, pt: (pt[b, n_pages - PAGES_PER_STEP + k], 0, 0))

    grid_spec = pltpu.PrefetchScalarGridSpec(
        num_scalar_prefetch=1,
        grid=(bd,),
        in_specs=[pl.BlockSpec((1, t_new, width), lambda b, pt: (b, 0, 0))]
        + [page_spec(k) for k in range(PAGES_PER_STEP)],
        out_specs=pl.BlockSpec((1, 1, LANES), lambda b, pt: (b, 0, 0)),
    )
    deep = pl.pallas_call(
        functools.partial(_sb_reach_kernel, n_heads=width // HEAD_DIM),
        grid_spec=grid_spec,
        out_shape=jax.ShapeDtypeStruct((bd, 1, LANES), jnp.int32),
        compiler_params=_compiler_params(("parallel",)),
        name="sb_reach",
    )(page_table, qb, *([ck_b] * PAGES_PER_STEP))
    return deep[:, 0, 0]


def _sample_kernel(pt_ref, deep_ref, qa_ref, qb_ref, kan_ref, van_ref, kbn_ref, vbn_ref, bias_ref, tri_ref,
                   *rest, t_new, n_heads, n_past_blocks):
    pps = PAGES_PER_STEP
    ka_pages, va_pages = rest[0:pps], rest[pps:2 * pps]
    kb_pages, vb_pages = rest[2 * pps:3 * pps], rest[3 * pps:4 * pps]
    oa_ref, ob_ref = rest[4 * pps], rest[4 * pps + 1]
    (qab_sc, qbb_sc, oblk_sc, gate_sc, max_sc, sum_sc, carry_sc, accb_sc) = rest[4 * pps + 2:]

    g = pl.program_id(1)
    rows = n_heads * t_new
    width = qa_ref.shape[2]
    n_blk = pps // PAGES_PER_BLOCK
    own = n_past_blocks
    lane = lax.broadcasted_iota(jnp.int32, (rows, LANES), 1)

    def record(entries):
        for ref, pos in ((gate_sc, 1), (max_sc, 2), (sum_sc, 3)):
            new = ref[...]
            for e in entries:
                new = jnp.where(lane == e[0], e[pos], new)
            ref[...] = new

    def softmax_partial(s, bias, mask):
        s = s[0:rows] + s[rows:2 * rows]
        gate = jnp.sum(s, axis=1, keepdims=True)
        s = s + bias
        if mask is not None:
            s = jnp.where(mask, s, -jnp.inf)
        m = jnp.max(s, axis=1, keepdims=True)
        p = jnp.exp(s - m)
        return p.astype(BF16), gate, m, jnp.sum(p, axis=1, keepdims=True)

    @pl.when(g == 0)
    def _init():
        qa_hi, qa_lo = _split_bf16(_expand_heads(qa_ref[0], n_heads))
        qab_sc[0:rows] = qa_hi
        qab_sc[rows:2 * rows] = qa_lo
        qb = _expand_heads(qb_ref[0], n_heads).astype(BF16)
        qbb_sc[...] = qb
        for ref in (gate_sc, max_sc, sum_sc):
            ref[...] = jnp.zeros_like(ref)

        def page_of(x):
            return jnp.concatenate([x, jnp.zeros((PAGE_SIZE - t_new, width), F32)], axis=0).astype(BF16)

        r_t = lax.broadcasted_iota(jnp.int32, (rows, PAGE_SIZE), 0) % t_new
        key = lax.broadcasted_iota(jnp.int32, (rows, PAGE_SIZE), 1)
        p, gate, m, l = softmax_partial(_dot_nt(qab_sc[...], page_of(kan_ref[0])),
                                        bias_ref[2, :, 0:PAGE_SIZE], key <= r_t)
        oblk_sc[own] = _dot(p, page_of(van_ref[0]))
        record([(own, gate, m, l)])

        past = key < r_t
        z = _dot_nt(qb, page_of(kbn_ref[0]))
        hi, lo = _split_bf16(jnp.where(past, _softplus2(z), 0.0))
        tri_page = tri_ref[0:PAGE_SIZE, 0:PAGE_SIZE]
        incl = _dot(hi, tri_page) + _dot(lo, tri_page)
        a = jnp.where(past, jnp.exp2(z - incl), 0.0)
        accb_sc[...] = _dot(a.astype(BF16), page_of(vbn_ref[0]))
        carry_sc[...] = -incl[:, 0:1]

    first_block = n_past_blocks - n_blk * (g + 1)
    cat = lambda refs: jnp.concatenate([r[0] for r in refs], axis=1).astype(BF16)
    keys_of = lambda c: slice(c * MOBA_BLOCK, (c + 1) * MOBA_BLOCK)

    @pl.when((jnp.max(carry_sc[...]) > F32_EXP2_UNDERFLOW) & ((g == 0) | (deep_ref[pl.program_id(0)] > 0)))
    def _stick_breaking():
        z = _dot(qbb_sc[...], cat(kb_pages))
        sp = _softplus2(z)
        stacked = jnp.concatenate([sp[:, keys_of(c)] for c in range(n_blk)], axis=0)
        a, carry = _sb_weights(z, _suffix_sums(stacked, tri_ref[...]), carry_sc[...], n_blk)
        carry_sc[...] = carry
        accb_sc[...] += _dot_nt(a.astype(BF16), cat(vb_pages))

    s_all = _dot(qab_sc[...], cat(ka_pages))
    va = cat(va_pages)
    entries = []
    for c in range(n_blk):
        j = first_block + c
        bias = bias_ref[jnp.where(j == n_past_blocks - 1, 1, 0)]
        p, gate, m, l = softmax_partial(s_all[:, keys_of(c)], bias, None)
        oblk_sc[j] = _dot_nt(p, va[:, keys_of(c)])
        entries.append((j, gate, m, l))
    record(entries)

    @pl.when(g == pl.num_programs(1) - 1)
    def _finish():
        sel = _top_blocks(gate_sc[...], lane < n_past_blocks, lane, n_past_blocks) | (lane == own)
        m_blk = max_sc[...]
        m_fin = jnp.max(jnp.where(sel, m_blk, -jnp.inf), axis=1, keepdims=True)
        wgt = jnp.where(sel, jnp.exp(m_blk - m_fin), 0.0)
        denom = jnp.sum(wgt * sum_sc[...], axis=1, keepdims=True)
        o = wgt[:, 0:1] * oblk_sc[0]
        for j in range(1, own + 1):
            o = o + wgt[:, j:j + 1] * oblk_sc[j]
        o = o / denom
        r_h = lax.broadcasted_iota(jnp.int32, (rows, width), 0) // t_new
        c_h = lax.broadcasted_iota(jnp.int32, (rows, width), 1) // HEAD_DIM
        own_head = r_h == c_h

        def fold(x):
            x = jnp.where(own_head, x, 0.0)
            out = x[0:t_new]
            for h in range(1, n_heads):
                out = out + x[h * t_new:(h + 1) * t_new]
            return out

        oa_ref[0] = fold(o)
        ob_ref[0] = fold(accb_sc[...])


def _sample_mixers(page_table, deep, qa, qb, ka_new, va_new, kb_new, vb_new,
                   ck_a, cv_a, ck_b, cv_b, bias_tabs, tri):
    bd, t_new, width = qa.shape
    n_heads = width // HEAD_DIM
    n_pages = page_table.shape[1]
    n_past_blocks = n_pages // PAGES_PER_BLOCK
    steps = n_pages // PAGES_PER_STEP
    rows = n_heads * t_new
    assert n_past_blocks + 1 <= LANES and n_pages % PAGES_PER_STEP == 0

    tok_spec = pl.BlockSpec((1, t_new, width), lambda b, g, pt, deep: (b, 0, 0))

    def page_spec(k, every_step):
        def index(b, g, pt, deep):
            step = g if every_step else jnp.where(deep[b] > 0, g, 0)
            return (pt[b, n_pages - PAGES_PER_STEP * (step + 1) + k], 0, 0)
        return pl.BlockSpec((1, width, PAGE_SIZE), index)

    moba_pages = [page_spec(k, True) for k in range(PAGES_PER_STEP)]
    sb_pages = [page_spec(k, False) for k in range(PAGES_PER_STEP)]
    const = lambda shape: pl.BlockSpec(shape, lambda b, g, pt, deep: (0,) * len(shape),
                                       pipeline_mode=pl.Buffered(1))
    grid_spec = pltpu.PrefetchScalarGridSpec(
        num_scalar_prefetch=2,
        grid=(bd, steps),
        in_specs=([tok_spec] * 6 + [const(bias_tabs.shape), const(tri.shape)]
                  + moba_pages * 2 + sb_pages * 2),
        out_specs=[tok_spec, tok_spec],
        scratch_shapes=[pltpu.VMEM((2 * rows, width), BF16), pltpu.VMEM((rows, width), BF16),
                        pltpu.VMEM((n_past_blocks + 1, rows, width), F32),
                        pltpu.VMEM((rows, LANES), F32), pltpu.VMEM((rows, LANES), F32),
                        pltpu.VMEM((rows, LANES), F32), pltpu.VMEM((rows, 1), F32),
                        pltpu.VMEM((rows, width), F32)],
    )
    out = jax.ShapeDtypeStruct((bd, t_new, width), F32)
    pages = ([ck_a] * PAGES_PER_STEP + [cv_a] * PAGES_PER_STEP
             + [ck_b] * PAGES_PER_STEP + [cv_b] * PAGES_PER_STEP)
    return pl.pallas_call(
        functools.partial(_sample_kernel, t_new=t_new, n_heads=n_heads, n_past_blocks=n_past_blocks),
        grid_spec=grid_spec,
        out_shape=[out, out],
        compiler_params=_compiler_params(("parallel", "arbitrary")),
        name="sample_mixers",
    )(page_table, deep, qa, qb, ka_new, va_new, kb_new, vb_new, bias_tabs, tri, *pages)


def _mlp_kernel(x_ref, oa_ref, ob_ref, ga_ref, gb_ref, wa_ref, wb_ref, wo_ref, ng_ref, wu_ref, wd_ref,
                y_ref):
    branch_a = _dot(oa_ref[...].astype(BF16), wa_ref[...])
    branch_b = _dot(ob_ref[...].astype(BF16), wb_ref[...])
    mixed = jax.nn.sigmoid(ga_ref[...]) * branch_a + jax.nn.sigmoid(gb_ref[...]) * branch_b
    x = x_ref[...] + _dot(mixed.astype(BF16), wo_ref[...])
    ms = jnp.mean(x * x, axis=-1, keepdims=True)
    h = (x * lax.rsqrt(ms + EPS) * ng_ref[...]).astype(BF16)
    up = jnp.maximum(_dot(h, wu_ref[...]), 0.0)
    y_ref[...] = x + _dot((up * up).astype(BF16), wd_ref[...])


def _merge_mlp(x, oa, ob, ga, gb, wa, wb, wo, ng, wu, wd):
    n, d_model = x.shape
    ts = TOKEN_TILE
    row = lambda w: pl.BlockSpec((ts, w), lambda i: (i, 0))
    return pl.pallas_call(
        _mlp_kernel,
        grid=(n // ts,),
        in_specs=[row(d_model), row(oa.shape[1]), row(ob.shape[1]), row(d_model), row(d_model),
                  _const_spec(wa.shape), _const_spec(wb.shape), _const_spec(wo.shape),
                  _const_spec(ng.shape), _const_spec(wu.shape), _const_spec(wd.shape)],
        out_specs=row(d_model),
        out_shape=jax.ShapeDtypeStruct((n, d_model), F32),
        compiler_params=_compiler_params(("parallel",)),
        name="merge_mlp",
    )(x, oa, ob, ga, gb, wa, wb, wo, ng, wu, wd)


def kernel(x_prompt, x_sample, cache_k_a, cache_v_a, cache_k_b, cache_v_b, page_table, rel_bias,
           norm_attn, w_in, q_norm, k_norm, w_br_a, w_br_b, w_o, norm_mlp, w_up, w_down):
    depth = w_in.shape[0]
    batch, seq, d_model = x_prompt.shape
    dec_batch, t_new, _ = x_sample.shape
    n_heads_a = cache_k_a.shape[3]
    n_heads_b = cache_k_b.shape[3]
    w_a, w_b = n_heads_a * HEAD_DIM, n_heads_b * HEAD_DIM
    assert w_a == w_b and n_heads_a == n_heads_b
    n_phys = cache_k_a.shape[1]
    past_len = page_table.shape[1] * PAGE_SIZE
    assert seq % MOBA_BLOCK == 0 and past_len % MOBA_BLOCK == 0 and t_new <= PAGE_SIZE
    assert TOKEN_TILE == MOBA_BLOCK and seq % QUERY_TILE == 0

    relb_t = rel_bias.T.astype(F32)
    bias_prompt = _prompt_bias(relb_t)
    far_prompt = jnp.broadcast_to(relb_t[:, N_BUCKETS - 1][:, None, None], (n_heads_a, 1, MOBA_BLOCK))
    bias_sample = _sample_bias(relb_t.reshape(-1), n_heads_a, t_new, past_len)
    head_of = np.arange(w_a) // HEAD_DIM
    gmat = jnp.asarray((head_of[:, None] == head_of[None, :]) / HEAD_DIM, BF16)
    idx = np.arange(MOBA_BLOCK)
    tri = idx[:, None] >= idx[None, :]
    tri = jnp.asarray(np.concatenate([tri, tri], axis=0), BF16)

    xp = x_prompt.reshape(batch * seq, d_model)
    xs = x_sample.reshape(dec_batch * t_new, d_model)
    prompt_kv, sample_kv = [], []
    for layer in range(depth):
        ng = norm_attn[layer][None]
        w_in_bf = w_in[layer].astype(BF16)
        qn_t = jnp.tile(q_norm[layer], n_heads_a)[None]
        kn_t = jnp.tile(k_norm[layer], n_heads_a)[None]
        mlp_w = (w_br_a[layer].astype(BF16), w_br_b[layer].astype(BF16), w_o[layer].astype(BF16),
                 norm_mlp[layer][None], w_up[layer].astype(BF16), w_down[layer].astype(BF16))

        (ka, va, kb, vb, ga, gb, qah, qal, kab, vab, qbb, kbb, vbb, km) = _proj(
            xp, ng, w_in_bf, gmat, qn_t, kn_t, w_a, w_b, seq_len=seq)
        seq3 = lambda t: t.reshape(batch, seq, t.shape[-1])
        oa = _moba_prompt(seq3(qah), seq3(qal), seq3(kab), seq3(vab),
                          km.reshape(batch, seq // MOBA_BLOCK, w_a), bias_prompt, far_prompt)
        ob = _sb_prompt(seq3(qbb), seq3(kbb), seq3(vbb), tri)
        xp = _merge_mlp(xp, oa.reshape(-1, w_a), ob.reshape(-1, w_b), ga, gb, *mlp_w)
        heads = lambda t, n: jnp.transpose(t.reshape(batch, n, HEAD_DIM, seq), (0, 3, 1, 2))
        prompt_kv.append((heads(ka, n_heads_a), heads(va, n_heads_a),
                          heads(kb, n_heads_b), heads(vb, n_heads_b)))

        (ka, va, kb, vb, ga, gb, qah, qal, _, _, qbb, _, _, _) = _proj(
            xs, ng, w_in_bf, gmat, qn_t, kn_t, w_a, w_b)
        tok3 = lambda t: t.reshape(dec_batch, t_new, t.shape[-1])
        qa_f32 = qah.astype(F32) + qal.astype(F32)
        pool = lambda c: jnp.transpose(c, (0, 1, 3, 4, 2)).reshape(depth * n_phys, -1, PAGE_SIZE)
        pages_of = page_table + layer * n_phys
        deep = _sb_reach(pages_of, tok3(qbb.astype(F32)), pool(cache_k_b))
        oa, ob = _sample_mixers(pages_of, deep, tok3(qa_f32), tok3(qbb.astype(F32)),
                                tok3(ka), tok3(va), tok3(kb), tok3(vb),
                                pool(cache_k_a), pool(cache_v_a), pool(cache_k_b), pool(cache_v_b),
                                bias_sample, tri)
        xs = _merge_mlp(xs, oa.reshape(-1, w_a), ob.reshape(-1, w_b), ga, gb, *mlp_w)
        heads = lambda t, n: t.reshape(dec_batch, t_new, n, HEAD_DIM)
        sample_kv.append((heads(ka, n_heads_a), heads(va, n_heads_a),
                          heads(kb, n_heads_b), heads(vb, n_heads_b)))

    stack = lambda kvs, i: jnp.stack([kv[i] for kv in kvs])
    return (xp.reshape(batch, seq, d_model), xs.reshape(dec_batch, t_new, d_model),
            *(stack(prompt_kv, i) for i in range(4)), *(stack(sample_kv, i) for i in range(4)))
```
